```python
import functools
import math
import jax, jax.numpy as jnp
from jax import lax
import numpy as np


D_MODEL = 1024
BATCH = 2
SEQ = 8192
DEPTH = 4
DEC_BATCH = 32
DEC_SEQ = 4
PAST_LEN = 8192
PAGE_SIZE = 128

HEAD_DIM = 64
D_ATT = D_MODEL // 2
D_MLP = D_MODEL // 4
D_SSM = D_MODEL - D_ATT - D_MLP
D_MIX = D_ATT + D_MLP + D_SSM
H_ATT = D_ATT // HEAD_DIM
H_MLP = D_MLP // HEAD_DIM
Q_BLOCK = 128
CHUNK = 128
SSM_CH = 16
SSM_G = D_SSM // SSM_CH
SSM_N = 64
D_FF = ((-(-8 * D_MODEL // 3) + 255) // 256) * 256
PLE_DIM = 256
D_IN = 3 * D_ATT + H_ATT + 2 * D_MLP + D_SSM
SPLITS = (D_ATT, 2 * D_ATT, 3 * D_ATT, 3 * D_ATT + H_ATT, 3 * D_ATT + H_ATT + D_MLP, 3 * D_ATT + H_ATT + 2 * D_MLP)
EPS = 1e-6
DT_MIN = 1e-3
DT_MAX = 1e-1
FORGET_BIAS = 3.0

kernel_name = 'fox_gmlp_s5_hybrid_step'


def rms_norm(x, g):
    xf = x.astype(jnp.float32)
    xf = xf * lax.rsqrt(jnp.mean(xf * xf, axis=-1, keepdims=True) + EPS)
    return (xf * g.astype(jnp.float32)).astype(x.dtype)


def project_in(h, g_norm, w_in, b_f, g_q, g_k):
    B, T, _ = h.shape
    z = rms_norm(h, g_norm) @ w_in
    q, k, v, fl, mu, mv, su = jnp.split(z, SPLITS, axis=-1)
    q = rms_norm(q.reshape(B, T, H_ATT, HEAD_DIM), g_q)
    k = rms_norm(k.reshape(B, T, H_ATT, HEAD_DIM), g_k)
    v = v.reshape(B, T, H_ATT, HEAD_DIM)
    logf = jax.nn.log_sigmoid((fl + b_f).astype(jnp.float32))
    return q, k, v, logf, jax.nn.gelu(mu), jax.nn.gelu(mv), su


def fox_attend(q, k, v, cq, ck, q_pos, k_pos):
    s = jnp.einsum('bqhd,bkhd->bhqk', q, k).astype(jnp.float32) * (HEAD_DIM ** -0.5)
    s = s + jnp.swapaxes(cq, 1, 2)[:, :, :, None] - jnp.swapaxes(ck, 1, 2)[:, :, None, :]
    mask = k_pos[None, :] <= q_pos[:, None]
    s = jnp.where(mask[None, None], s, -jnp.inf)
    p = jax.nn.softmax(s, axis=-1)
    return jnp.einsum('bhqk,bkhd->bqhd', p.astype(v.dtype), v)


def fox_prompt(q, k, v, logf):
    B, T = q.shape[:2]
    nb = T // Q_BLOCK
    c = jnp.cumsum(logf, axis=1)
    pos = jnp.arange(T, dtype=jnp.int32)
    qb = jnp.swapaxes(q.reshape(B, nb, Q_BLOCK, H_ATT, HEAD_DIM), 0, 1)
    cb = jnp.swapaxes(c.reshape(B, nb, Q_BLOCK, H_ATT), 0, 1)
    pb = pos.reshape(nb, Q_BLOCK)
    out = lax.map(lambda blk: fox_attend(blk[0], k, v, blk[1], c, blk[2], pos), (qb, cb, pb))
    return jnp.swapaxes(out, 0, 1).reshape(B, T, D_ATT)


def fox_sample(q, k, v, logf, k_past, v_past, logf_past):
    DB, DS = q.shape[:2]
    past_len = k_past.shape[1]
    k_all = jnp.concatenate([k_past.astype(k.dtype), k], axis=1)
    v_all = jnp.concatenate([v_past.astype(v.dtype), v], axis=1)
    c_all = jnp.cumsum(jnp.concatenate([logf_past.astype(jnp.float32), logf], axis=1), axis=1)
    q_pos = past_len + jnp.arange(DS, dtype=jnp.int32)
    k_pos = jnp.arange(past_len + DS, dtype=jnp.int32)
    out = fox_attend(q, k_all, v_all, c_all[:, past_len:], c_all, q_pos, k_pos)
    return out.reshape(DB, DS, D_ATT)


def chunk_mlp_prompt(mu, mv, w_s, b_s):
    B, T, _ = mu.shape
    nc = T // CHUNK
    ws = w_s * jnp.tril(jnp.ones((CHUNK, CHUNK), w_s.dtype))
    vv = mv.reshape(B, nc, CHUNK, H_MLP, HEAD_DIM)
    mix = jnp.einsum('hij,bcjhd->bcihd', ws, vv) + b_s.T[:, :, None]
    return mu * mix.reshape(B, T, D_MLP)


def chunk_mlp_sample(mu, mv, w_s, b_s):
    DB, DS, _ = mu.shape
    ws = (w_s * jnp.tril(jnp.ones((CHUNK, CHUNK), w_s.dtype)))[:, :DS, :DS]
    vv = mv.reshape(DB, DS, H_MLP, HEAD_DIM)
    mix = jnp.einsum('hij,bjhd->bihd', ws, vv) + b_s[:, :DS].T[:, :, None]
    return mu * mix.reshape(DB, DS, D_MLP)


def _complex_combine(e1, e2):
    a1r, a1i, b1r, b1i = e1
    a2r, a2i, b2r, b2i = e2
    return (a2r * a1r - a2i * a1i,
            a2r * a1i + a2i * a1r,
            a2r * b1r - a2i * b1i + b2r,
            a2r * b1i + a2i * b1r + b2i)


def s5_mixer(u, h0_re, h0_im, a_re, a_im, log_dt, b_re, b_im, c_re, c_im, d, w_glu, b_glu):
    f32 = jnp.float32
    B, T, _ = u.shape
    uf = u.astype(f32).reshape(B, T, SSM_G, SSM_CH)
    a_re, a_im = a_re.astype(f32), a_im.astype(f32)
    b_re, b_im = b_re.astype(f32), b_im.astype(f32)
    dt = jnp.exp(log_dt.astype(f32))[:, None]
    mag = jnp.exp(a_re * dt)
    ab_re = mag * jnp.cos(a_im * dt)
    ab_im = mag * jnp.sin(a_im * dt)
    den = a_re * a_re + a_im * a_im
    coef_re = ((ab_re - 1.0) * a_re + ab_im * a_im) / den
    coef_im = (ab_im * a_re - (ab_re - 1.0) * a_im) / den
    bb_re = coef_re[..., None] * b_re - coef_im[..., None] * b_im
    bb_im = coef_re[..., None] * b_im + coef_im[..., None] * b_re
    bu_re = jnp.einsum('gnc,btgc->btgn', bb_re, uf)
    bu_im = jnp.einsum('gnc,btgc->btgn', bb_im, uf)
    h0_re, h0_im = h0_re.astype(f32), h0_im.astype(f32)
    bu_re = bu_re.at[:, 0].add(ab_re * h0_re - ab_im * h0_im)
    bu_im = bu_im.at[:, 0].add(ab_re * h0_im + ab_im * h0_re)
    shp = bu_re.shape
    _, _, h_re, h_im = lax.associative_scan(
        _complex_combine,
        (jnp.broadcast_to(ab_re, shp), jnp.broadcast_to(ab_im, shp), bu_re, bu_im), axis=1)
    y = (jnp.einsum('gcn,btgn->btgc', c_re.astype(f32), h_re)
         - jnp.einsum('gcn,btgn->btgc', c_im.astype(f32), h_im)
         + d.astype(f32).reshape(SSM_G, SSM_CH) * uf)
    z = jax.nn.gelu(y)
    gate = jax.nn.sigmoid(jnp.einsum('gcd,btgc->btgd', w_glu.astype(f32), z) + b_glu.astype(f32).reshape(SSM_G, SSM_CH))
    out = (z * gate).reshape(B, T, D_SSM).astype(u.dtype)
    return out, h_re[:, -1], h_im[:, -1]


def merge_heads(att, mlp, ssm, g_out, w_out):
    parts = jnp.concatenate([
        rms_norm(att, g_out[:D_ATT]),
        rms_norm(mlp, g_out[D_ATT:D_ATT + D_MLP]),
        rms_norm(ssm, g_out[D_ATT + D_MLP:])], axis=-1)
    return parts @ w_out


def trunk_layer(h, pe, W, i, att_core, mlp_core, h0_re, h0_im):
    q, k, v, logf, mu, mv, su = project_in(h, W['g_norm_mix'][i], W['w_in'][i], W['b_forget'][i], W['g_q'][i], W['g_k'][i])
    att = att_core(q, k, v, logf)
    mlp = mlp_core(mu, mv, W['w_spatial'][i], W['b_spatial'][i])
    ssm, h_re, h_im = s5_mixer(su, h0_re, h0_im, W['ssm_a_re'][i], W['ssm_a_im'][i], W['ssm_log_dt'][i],
                               W['ssm_b_re'][i], W['ssm_b_im'][i], W['ssm_c_re'][i], W['ssm_c_im'][i],
                               W['ssm_d'][i], W['w_glu'][i], W['b_glu'][i])
    h = h + merge_heads(att, mlp, ssm, W['g_mix_out'][i], W['w_out'][i])
    hn = rms_norm(h, W['g_norm_ffn'][i])
    h = h + (jax.nn.silu(hn @ W['w_ffn_gate'][i]) * (hn @ W['w_ffn_up'][i])) @ W['w_ffn_down'][i]
    gate = jax.nn.sigmoid(rms_norm(h, W['g_norm_ple'][i]) @ W['w_ple_gate'][i])
    h = h + (pe @ W['w_ple_proj'][i]) * gate
    return h, (k, v, logf, h_re, h_im, mv)


def setup_inputs(seed: int = 0) -> dict:
    key = jax.random.key(seed)
    ks = jax.random.split(key, 40)
    n_pages = PAST_LEN // PAGE_SIZE
    n_used = DEC_BATCH * n_pages
    n_pool = n_used + n_used // 4

    def nrm(k, shape, scale):
        return jax.random.normal(k, shape, jnp.float32) * scale

    inp = {}
    inp['x_prompt'] = nrm(ks[0], (BATCH, SEQ, D_MODEL), 1.0)
    inp['x_sample'] = nrm(ks[1], (DEC_BATCH, DEC_SEQ, D_MODEL), 1.0)
    inp['cache_k'] = nrm(ks[2], (DEPTH, n_pool, PAGE_SIZE, H_ATT, HEAD_DIM), 1.0)
    inp['cache_v'] = nrm(ks[3], (DEPTH, n_pool, PAGE_SIZE, H_ATT, HEAD_DIM), 1.0)
    inp['cache_logf'] = jax.nn.log_sigmoid(FORGET_BIAS + nrm(ks[4], (DEPTH, n_pool, PAGE_SIZE, H_ATT), 1.0))
    inp['state_ssm_re'] = nrm(ks[5], (DEPTH, DEC_BATCH, SSM_G, SSM_N), 0.3)
    inp['state_ssm_im'] = nrm(ks[6], (DEPTH, DEC_BATCH, SSM_G, SSM_N), 0.3)
    inp['page_table'] = jax.random.permutation(ks[7], n_pool)[:n_used].reshape(DEC_BATCH, n_pages).astype(jnp.int32)
    inp['p_prompt'] = nrm(ks[8], (DEPTH, BATCH, SEQ, PLE_DIM), 1.0)
    inp['p_sample'] = nrm(ks[9], (DEPTH, DEC_BATCH, DEC_SEQ, PLE_DIM), 1.0)
    inp['g_norm_mix'] = 1.0 + nrm(ks[10], (DEPTH, D_MODEL), 0.02)
    inp['w_in'] = nrm(ks[11], (DEPTH, D_MODEL, D_IN), D_MODEL ** -0.5)
    inp['b_forget'] = FORGET_BIAS + nrm(ks[12], (DEPTH, H_ATT), 0.5)
    inp['g_q'] = 1.0 + nrm(ks[13], (DEPTH, HEAD_DIM), 0.02)
    inp['g_k'] = 1.0 + nrm(ks[14], (DEPTH, HEAD_DIM), 0.02)
    inp['w_spatial'] = nrm(ks[15], (DEPTH, H_MLP, CHUNK, CHUNK), 0.5 * CHUNK ** -0.5)
    inp['b_spatial'] = 1.0 + nrm(ks[16], (DEPTH, H_MLP, CHUNK), 0.1)
    inp['ssm_a_re'] = -0.5 + nrm(ks[17], (DEPTH, SSM_G, SSM_N), 0.01)
    inp['ssm_a_im'] = math.pi * jnp.arange(SSM_N, dtype=jnp.float32)[None, None, :] + nrm(ks[18], (DEPTH, SSM_G, SSM_N), 0.01)
    inp['ssm_log_dt'] = jax.random.uniform(ks[19], (DEPTH, SSM_G), jnp.float32, math.log(DT_MIN), math.log(DT_MAX))
    inp['ssm_b_re'] = nrm(ks[20], (DEPTH, SSM_G, SSM_N, SSM_CH), (2 * SSM_CH) ** -0.5)
    inp['ssm_b_im'] = nrm(ks[21], (DEPTH, SSM_G, SSM_N, SSM_CH), (2 * SSM_CH) ** -0.5)
    inp['ssm_c_re'] = nrm(ks[22], (DEPTH, SSM_G, SSM_CH, SSM_N), SSM_N ** -0.5)
    inp['ssm_c_im'] = nrm(ks[23], (DEPTH, SSM_G, SSM_CH, SSM_N), SSM_N ** -0.5)
    inp['ssm_d'] = nrm(ks[24], (DEPTH, D_SSM), 1.0)
    inp['w_glu'] = nrm(ks[25], (DEPTH, SSM_G, SSM_CH, SSM_CH), SSM_CH ** -0.5)
    inp['b_glu'] = nrm(ks[26], (DEPTH, D_SSM), 0.02)
    inp['g_mix_out'] = 1.0 + nrm(ks[27], (DEPTH, D_MIX), 0.02)
    inp['w_out'] = nrm(ks[28], (DEPTH, D_MIX, D_MODEL), 0.5 * D_MIX ** -0.5)
    inp['g_norm_ffn'] = 1.0 + nrm(ks[29], (DEPTH, D_MODEL), 0.02)
    inp['w_ffn_gate'] = nrm(ks[30], (DEPTH, D_MODEL, D_FF), D_MODEL ** -0.5)
    inp['w_ffn_up'] = nrm(ks[31], (DEPTH, D_MODEL, D_FF), D_MODEL ** -0.5)
    inp['w_ffn_down'] = nrm(ks[32], (DEPTH, D_FF, D_MODEL), 0.5 * D_FF ** -0.5)
    inp['g_norm_ple'] = 1.0 + nrm(ks[33], (DEPTH, D_MODEL), 0.02)
    inp['w_ple_gate'] = nrm(ks[34], (DEPTH, D_MODEL, D_MODEL), D_MODEL ** -0.5)
    inp['w_ple_proj'] = nrm(ks[35], (DEPTH, PLE_DIM, D_MODEL), 0.5 * PLE_DIM ** -0.5)
    return inp


def reference(x_prompt, x_sample, cache_k, cache_v, cache_logf, state_ssm_re, state_ssm_im, page_table,
              p_prompt, p_sample, g_norm_mix, w_in, b_forget, g_q, g_k, w_spatial, b_spatial,
              ssm_a_re, ssm_a_im, ssm_log_dt, ssm_b_re, ssm_b_im, ssm_c_re, ssm_c_im, ssm_d, w_glu, b_glu,
              g_mix_out, w_out, g_norm_ffn, w_ffn_gate, w_ffn_up, w_ffn_down, g_norm_ple, w_ple_gate, w_ple_proj):
    W = dict(g_norm_mix=g_norm_mix, w_in=w_in, b_forget=b_forget, g_q=g_q, g_k=g_k,
             w_spatial=w_spatial, b_spatial=b_spatial, ssm_a_re=ssm_a_re, ssm_a_im=ssm_a_im,
             ssm_log_dt=ssm_log_dt, ssm_b_re=ssm_b_re, ssm_b_im=ssm_b_im, ssm_c_re=ssm_c_re,
             ssm_c_im=ssm_c_im, ssm_d=ssm_d, w_glu=w_glu, b_glu=b_glu, g_mix_out=g_mix_out,
             w_out=w_out, g_norm_ffn=g_norm_ffn, w_ffn_gate=w_ffn_gate, w_ffn_up=w_ffn_up,
             w_ffn_down=w_ffn_down, g_norm_ple=g_norm_ple, w_ple_gate=w_ple_gate, w_ple_proj=w_ple_proj)
    n_dec = x_sample.shape[0]
    past_len = page_table.shape[1] * PAGE_SIZE
    h0 = jnp.zeros((x_prompt.shape[0], SSM_G, SSM_N), jnp.float32)
    hp, hs = x_prompt, x_sample
    kp_l, vp_l, lfp_l, srp_l, sip_l = [], [], [], [], []
    ks_l, vs_l, lfs_l, srs_l, sis_l, mvs_l = [], [], [], [], [], []
    for i in range(DEPTH):
        hp, (k_, v_, lf_, sr_, si_, _) = trunk_layer(hp, p_prompt[i], W, i, fox_prompt, chunk_mlp_prompt, h0, h0)
        kp_l.append(k_); vp_l.append(v_); lfp_l.append(lf_); srp_l.append(sr_); sip_l.append(si_)
        k_past = cache_k[i, page_table].reshape(n_dec, past_len, H_ATT, HEAD_DIM)
        v_past = cache_v[i, page_table].reshape(n_dec, past_len, H_ATT, HEAD_DIM)
        lf_past = cache_logf[i, page_table].reshape(n_dec, past_len, H_ATT)
        att_s = functools.partial(fox_sample, k_past=k_past, v_past=v_past, logf_past=lf_past)
        hs, (k_, v_, lf_, sr_, si_, mv_) = trunk_layer(hs, p_sample[i], W, i, att_s, chunk_mlp_sample,
                                                       state_ssm_re[i], state_ssm_im[i])
        ks_l.append(k_); vs_l.append(v_); lfs_l.append(lf_); srs_l.append(sr_); sis_l.append(si_); mvs_l.append(mv_)
    k_prompt = jnp.stack(kp_l); v_prompt = jnp.stack(vp_l); logf_prompt = jnp.stack(lfp_l)
    ssm_re_prompt = jnp.stack(srp_l); ssm_im_prompt = jnp.stack(sip_l)
    k_sample = jnp.stack(ks_l); v_sample = jnp.stack(vs_l); logf_sample = jnp.stack(lfs_l)
    ssm_re_sample = jnp.stack(srs_l); ssm_im_sample = jnp.stack(sis_l); mlp_v_sample = jnp.stack(mvs_l)
    return (hp, hs, k_prompt, v_prompt, logf_prompt, ssm_re_prompt, ssm_im_prompt,
            k_sample, v_sample, logf_sample, ssm_re_sample, ssm_im_sample, mlp_v_sample)
```

```python
import functools

import jax
import jax.numpy as jnp
from jax import lax
from jax.experimental import pallas as pl
from jax.experimental.pallas import tpu as pltpu

F32 = jnp.float32
BF16 = jnp.bfloat16

HEAD_DIM = 64
CHUNK = 128
SSM_CH = 16
EPS = 1e-6
LANES = 128
BF16_ROWS = 16
NEG = -1e30

VMEM_LIMIT = 52 * 1024 * 1024
HIGHEST = lax.Precision.HIGHEST
NT_DIMS = (((1,), (1,)), ((), ()))


def _cparams(*sem):
    return pltpu.CompilerParams(dimension_semantics=sem, vmem_limit_bytes=VMEM_LIMIT)


def _rms(x):
    return x * lax.rsqrt(jnp.mean(x * x, axis=-1, keepdims=True) + EPS)


def _log_sigmoid(x):
    return jnp.minimum(x, 0.0) - jnp.log1p(jnp.exp(-jnp.abs(x)))


def _const_spec(shape):
    return pl.BlockSpec(shape, lambda *_: (0,) * len(shape), pipeline_mode=pl.Buffered(1))


def _proj_in_kernel(*refs, d_att, d_mlp, d_ssm, tm, feature_major):
    if feature_major:
        (h_ref, g_ref, w1_ref, w2t_ref, bf_ref, gq_ref, gk_ref, gavg_ref, ws_ref, bs_ref,
         q_ref, kt_ref, vt_ref, ktb_ref, vtb_ref, lf_ref, mlp_ref, su_ref) = refs
    else:
        (h_ref, g_ref, w1_ref, bf_ref, gq_ref, gk_ref, gavg_ref, ws_ref, bs_ref,
         q_ref, k_ref, v_ref, lf_ref, mlp_ref, su_ref, mv_ref) = refs
    n_heads = d_att // HEAD_DIM
    xn = (_rms(h_ref[...]) * g_ref[...]).astype(BF16)
    z = jnp.dot(xn, w1_ref[...], preferred_element_type=F32)
    q = z[:, :d_att]
    mu = jax.nn.gelu(z[:, d_att:d_att + d_mlp])
    mv = jax.nn.gelu(z[:, d_att + d_mlp:d_att + 2 * d_mlp])
    o = d_att + 2 * d_mlp + d_ssm
    su_ref[...] = z[:, d_att + 2 * d_mlp:o]

    gavg = gavg_ref[...]

    def head_norm(t, g):
        ms = jnp.dot((t * t).astype(BF16), gavg, preferred_element_type=F32)
        return t * lax.rsqrt(ms + EPS) * g

    q_ref[...] = (head_norm(q, gq_ref[...]) * (HEAD_DIM ** -0.5)).astype(BF16)

    if feature_major:
        kvf = lax.dot_general(w2t_ref[...], xn, NT_DIMS, preferred_element_type=F32)
        k3 = kvf[:d_att].reshape(n_heads, HEAD_DIM, tm)
        ms = jnp.mean(k3 * k3, axis=1, keepdims=True)
        kn = (k3 * lax.rsqrt(ms + EPS)).reshape(d_att, tm) * gk_ref[...]
        vt = kvf[d_att:2 * d_att]
        kt_ref[0] = kn
        ktb_ref[0] = kn.astype(BF16)
        vt_ref[0] = vt
        vtb_ref[0] = vt.astype(BF16)
        lf_ref[0] = _log_sigmoid(kvf[2 * d_att:2 * d_att + n_heads] + bf_ref[...])
    else:
        k_ref[...] = head_norm(z[:, o:o + d_att], gk_ref[...])
        v_ref[...] = z[:, o + d_att:o + 2 * d_att]
        lf = _log_sigmoid(z[:, o + 2 * d_att:] + bf_ref[...])
        lf_ref[...] = lf.T[:n_heads, :]
        mv_ref[...] = mv

    n_mlp_heads = d_mlp // HEAD_DIM
    rows = lax.broadcasted_iota(jnp.int32, (CHUNK, CHUNK), 0)
    cols = lax.broadcasted_iota(jnp.int32, (CHUNK, CHUNK), 1)
    wst = [jnp.where(rows >= cols, ws_ref[h], 0.0).astype(BF16) for h in range(n_mlp_heads)]
    lane_head = lax.broadcasted_iota(jnp.int32, (CHUNK, d_mlp), 1) // HEAD_DIM
    for c in range(tm // CHUNK):
        sl = slice(c * CHUNK, (c + 1) * CHUNK)
        mvc = mv[sl].astype(BF16)
        mix = bs_ref[...]
        for h in range(n_mlp_heads):
            r = jnp.dot(wst[h], mvc, preferred_element_type=F32)
            mix = mix + jnp.where(lane_head == h, r, 0.0)
        mlp_ref[sl, :] = mu[sl] * mix


def _proj_in(h, g_norm, w1, w2t, b_f, gq, gk, gavg, ws, bs, *, tm, d_att, d_mlp, d_ssm, seq_len=None):
    n, d_model = h.shape
    n_heads = d_att // HEAD_DIM
    feature_major = w2t is not None
    assert n % tm == 0 and tm % CHUNK == 0
    row = lambda width: pl.BlockSpec((tm, width), lambda i: (i, 0))
    sds = jax.ShapeDtypeStruct
    if feature_major:
        batch, nt = n // seq_len, seq_len // tm
        assert seq_len % tm == 0
        col = lambda height: pl.BlockSpec((1, height, tm), lambda i: (i // nt, 0, i % nt))
        out_shape = [sds((n, d_att), BF16), sds((batch, d_att, seq_len), F32), sds((batch, d_att, seq_len), F32),
                     sds((batch, d_att, seq_len), BF16), sds((batch, d_att, seq_len), BF16),
                     sds((batch, n_heads, seq_len), F32), sds((n, d_mlp), F32), sds((n, d_ssm), F32)]
        out_specs = [row(d_att), col(d_att), col(d_att), col(d_att), col(d_att), col(n_heads), row(d_mlp), row(d_ssm)]
        consts = [g_norm, w1, w2t, b_f, gq, gk, gavg, ws, bs]
    else:
        out_shape = [sds((n, d_att), BF16), sds((n, d_att), F32), sds((n, d_att), F32), sds((n_heads, n), F32),
                     sds((n, d_mlp), F32), sds((n, d_ssm), F32), sds((n, d_mlp), F32)]
        out_specs = [row(d_att), row(d_att), row(d_att), pl.BlockSpec((n_heads, tm), lambda i: (0, i)),
                     row(d_mlp), row(d_ssm), row(d_mlp)]
        consts = [g_norm, w1, b_f, gq, gk, gavg, ws, bs]
    return pl.pallas_call(
        functools.partial(_proj_in_kernel, d_att=d_att, d_mlp=d_mlp, d_ssm=d_ssm, tm=tm, feature_major=feature_major),
        grid=(n // tm,),
        in_specs=[row(d_model)] + [_const_spec(c.shape) for c in consts],
        out_specs=out_specs,
        out_shape=out_shape,
        compiler_params=_cparams("parallel"),
        name="proj_in",
    )(h, *consts)


def _cumsum_kernel(x_ref, o_ref, *, nb):
    x = x_ref[...]
    lane = lax.broadcasted_iota(jnp.int32, x.shape, 1)
    y = x
    s = 1
    while s < LANES:
        y = y + jnp.where(lane >= s, pltpu.roll(y, s, 1), 0.0)
        s *= 2
    tot = jnp.broadcast_to(y[:, LANES - 1:LANES], x.shape)
    blk = lax.broadcasted_iota(jnp.int32, x.shape, 0) % nb
    z = tot
    s = 1
    while s < nb:
        z = z + jnp.where(blk >= s, pltpu.roll(z, s, 0), 0.0)
        s *= 2
    o_ref[...] = y + (z - tot)


def _cumsum_seq(lf):
    seq_len = lf.shape[-1]
    x = lf.reshape(lf.size // LANES, LANES)
    y = pl.pallas_call(
        functools.partial(_cumsum_kernel, nb=seq_len // LANES),
        out_shape=jax.ShapeDtypeStruct(x.shape, F32),
        compiler_params=pltpu.CompilerParams(vmem_limit_bytes=VMEM_LIMIT),
        name="cumsum_logf",
    )(x)
    return y.reshape(lf.shape)


def _attn_prompt_kernel(q_ref, kt_ref, vt_ref, c_ref, o_ref, m_sc, l_sc, acc_sc, *, tq):
    i = pl.program_id(2)
    lane = lax.broadcasted_iota(jnp.int32, (tq, LANES), 1)
    q = q_ref[...]
    zero = jnp.zeros_like(q)
    q_head = (jnp.where(lane < HEAD_DIM, q, zero), jnp.where(lane >= HEAD_DIM, q, zero))
    q0 = pl.multiple_of(i * tq, tq)
    ref_level = [c_ref[0, 0, e:e + 1, pl.ds(q0, LANES)][:, 0:1] for e in range(2)]

    m_sc[...] = jnp.full(m_sc.shape, NEG, F32)
    l_sc[...] = jnp.zeros(l_sc.shape, F32)
    acc_sc[...] = jnp.zeros(acc_sc.shape, F32)

    def step(j, masked):
        k0 = pl.multiple_of(j * tq, tq)
        kt = kt_ref[0, :, pl.ds(k0, tq)]
        vt = vt_ref[0, :, pl.ds(k0, tq)]
        for e in range(2):
            s = jnp.dot(q_head[e], kt, preferred_element_type=F32)
            s = s + (ref_level[e] - c_ref[0, 0, e:e + 1, pl.ds(k0, tq)])
            if masked:
                qpos = lax.broadcasted_iota(jnp.int32, (tq, tq), 0)
                kpos = lax.broadcasted_iota(jnp.int32, (tq, tq), 1)
                s = jnp.where(kpos <= qpos, s, NEG)
            m_old = m_sc[e]
            m_new = jnp.maximum(m_old, jnp.max(s, axis=1, keepdims=True))
            alpha = jnp.exp(m_old - m_new)
            p = jnp.exp(s - m_new)
            l_sc[e] = alpha * l_sc[e] + jnp.sum(p, axis=1, keepdims=True)
            acc_sc[e] = alpha * acc_sc[e] + lax.dot_general(p.astype(BF16), vt, NT_DIMS, preferred_element_type=F32)
            m_sc[e] = m_new

    def body(j, carry):
        step(j, False)
        return carry

    lax.fori_loop(0, i, body, 0)
    step(i, True)
    o_ref[...] = jnp.where(lane < HEAD_DIM, acc_sc[0] / l_sc[0], acc_sc[1] / l_sc[1])


def _attn_prompt(q_bf, kt_bf, vt_bf, c, *, tq):
    batch, d_att, seq_len = kt_bf.shape
    n_pairs = d_att // LANES
    nq = seq_len // tq
    assert seq_len % tq == 0 and tq % LANES == 0
    c4 = c.reshape(batch, n_pairs, 2, seq_len)
    kv_spec = pl.BlockSpec((1, LANES, seq_len), lambda b, hp, i: (b, hp, 0))
    return pl.pallas_call(
        functools.partial(_attn_prompt_kernel, tq=tq),
        grid=(batch, n_pairs, nq),
        in_specs=[pl.BlockSpec((tq, LANES), lambda b, hp, i: (b * nq + i, hp)), kv_spec, kv_spec,
                  pl.BlockSpec((1, 1, 2, seq_len), lambda b, hp, i: (b, hp, 0, 0))],
        out_specs=pl.BlockSpec((tq, LANES), lambda b, hp, i: (b * nq + i, hp)),
        out_shape=jax.ShapeDtypeStruct((batch * seq_len, d_att), F32),
        scratch_shapes=[pltpu.VMEM((2, tq, 1), F32), pltpu.VMEM((2, tq, 1), F32), pltpu.VMEM((2, tq, LANES), F32)],
        compiler_params=_cparams("parallel", "parallel", "parallel"),
        name="attn_prompt",
    )(q_bf, kt_bf, vt_bf, c4)


def _attn_sample_kernel(pt_ref, q_ref, kn_ref, vn_ref, lfn_ref, *refs, pages_per_step, n_new):
    del pt_ref
    pg = pages_per_step
    k_refs, v_refs, lf_refs = refs[:pg], refs[pg:2 * pg], refs[2 * pg:3 * pg]
    o_ref, qbd_sc, m_sc, l_sc, acc_sc, coff_sc = refs[3 * pg:]
    g = pl.program_id(1)
    n_rows, d_att = qbd_sc.shape
    n_heads = d_att // HEAD_DIM
    row_head = lax.broadcasted_iota(jnp.int32, (n_rows, d_att), 0) % n_heads
    lane_head = lax.broadcasted_iota(jnp.int32, (n_rows, d_att), 1) // HEAD_DIM
    own = row_head == lane_head

    @pl.when(g == 0)
    def _init():
        qf = q_ref[0].astype(F32)
        rows = jnp.concatenate([jnp.broadcast_to(qf[i:i + 1], (n_heads, d_att)) for i in range(n_new)], axis=0)
        qbd_sc[...] = jnp.where(own, rows, 0.0).astype(BF16)
        m_sc[...] = jnp.full(m_sc.shape, NEG, F32)
        l_sc[...] = jnp.zeros(l_sc.shape, F32)
        acc_sc[...] = jnp.zeros(acc_sc.shape, F32)
        coff_sc[...] = jnp.zeros(coff_sc.shape, F32)

    lane = lax.broadcasted_iota(jnp.int32, (n_rows, CHUNK), 1)
    coff = coff_sc[...]
    qbd = qbd_sc[...]
    s_parts = []
    for p in range(pg):
        lf = jnp.concatenate([lf_refs[p][0, 0]] * n_new, axis=0)
        s = 1
        while s < CHUNK:
            lf = lf + jnp.where(lane >= s, pltpu.roll(lf, s, 1), 0.0)
            s *= 2
        ck = coff + lf
        coff = coff + lf[:, CHUNK - 1:CHUNK]
        s_parts.append(jnp.dot(qbd, k_refs[p][0, 0].astype(BF16), preferred_element_type=F32) - ck)
    coff_sc[...] = coff
    s_all = jnp.concatenate(s_parts, axis=1)
    m_old = m_sc[...]
    m_new = jnp.maximum(m_old, jnp.max(s_all, axis=1, keepdims=True))
    alpha = jnp.exp(m_old - m_new)
    p_all = jnp.exp(s_all - m_new)
    l_sc[...] = alpha * l_sc[...] + jnp.sum(p_all, axis=1, keepdims=True)
    pv = jnp.zeros(acc_sc.shape, F32)
    for p in range(pg):
        pv = pv + lax.dot_general(p_all[:, p * CHUNK:(p + 1) * CHUNK].astype(BF16), v_refs[p][0, 0].astype(BF16),
                                  NT_DIMS, preferred_element_type=F32)
    acc_sc[...] = alpha * acc_sc[...] + pv
    m_sc[...] = m_new

    @pl.when(g == pl.num_programs(1) - 1)
    def _finish():
        qf = qbd_sc[...].astype(F32)
        row_query = lax.broadcasted_iota(jnp.int32, (n_rows, 1), 0) // n_heads
        lfn = jnp.concatenate([lfn_ref[0]] * n_new, axis=0)
        c = coff_sc[...]
        m, l, acc = m_sc[...], l_sc[...], acc_sc[...]
        for j in range(n_new):
            c = c + lfn[:, j:j + 1]
            s = jnp.sum(qf * kn_ref[0, j:j + 1, :], axis=1, keepdims=True) - c
            s = jnp.where(row_query >= j, s, NEG)
            m_new = jnp.maximum(m, s)
            alpha = jnp.exp(m - m_new)
            pj = jnp.exp(s - m_new)
            l = alpha * l + pj
            acc = alpha * acc + pj * vn_ref[0, j:j + 1, :]
            m = m_new
        out = jnp.where(own, acc / l, 0.0)
        for i in range(n_new):
            o_ref[0, i:i + 1, :] = jnp.sum(out[i * n_heads:(i + 1) * n_heads], axis=0, keepdims=True)


def _attn_sample(page_table, q_bf, k_new, v_new, lf_new, cache_kt, cache_vt, cache_lft, *, layer, pages_per_step):
    n_seq, n_new, d_att = q_bf.shape
    n_heads = d_att // HEAD_DIM
    n_pages = page_table.shape[1]
    pg = pages_per_step
    assert n_pages % pg == 0 and cache_kt.shape[2:] == (d_att, CHUNK) and cache_lft.shape[2:] == (n_heads, CHUNK)
    n_rows = n_new * n_heads

    def page_spec(height, p):
        return pl.BlockSpec((1, 1, height, CHUNK), lambda s, g, pt: (layer, pt[s, g * pg + p], 0, 0))

    seq_spec = lambda a: pl.BlockSpec((1,) + a.shape[1:], lambda s, g, pt: (s, 0, 0))
    grid_spec = pltpu.PrefetchScalarGridSpec(
        num_scalar_prefetch=1,
        grid=(n_seq, n_pages // pg),
        in_specs=[seq_spec(q_bf), seq_spec(k_new), seq_spec(v_new), seq_spec(lf_new)]
        + [page_spec(d_att, p) for p in range(pg)]
        + [page_spec(d_att, p) for p in range(pg)]
        + [page_spec(n_heads, p) for p in range(pg)],
        out_specs=pl.BlockSpec((1, n_new, d_att), lambda s, g, pt: (s, 0, 0)),
        scratch_shapes=[pltpu.VMEM((n_rows, d_att), BF16), pltpu.VMEM((n_rows, 1), F32), pltpu.VMEM((n_rows, 1), F32),
                        pltpu.VMEM((n_rows, d_att), F32), pltpu.VMEM((n_rows, 1), F32)],
    )
    return pl.pallas_call(
        functools.partial(_attn_sample_kernel, pages_per_step=pg, n_new=n_new),
        grid_spec=grid_spec,
        out_shape=jax.ShapeDtypeStruct((n_seq, n_new, d_att), F32),
        compiler_params=_cparams("parallel", "arbitrary"),
        name="attn_sample",
    )(page_table, q_bf, k_new, v_new, lf_new, *([cache_kt] * pg), *([cache_vt] * pg), *([cache_lft] * pg))


def _s5_tail(hs_bf, u, cc_ref, d_ref, wg_ref, bg_ref):
    y = jnp.dot(hs_bf, cc_ref[...], preferred_element_type=F32) + d_ref[...] * u
    z = jax.nn.gelu(y)
    gate = jax.nn.sigmoid(jnp.dot(z.astype(BF16), wg_ref[...], preferred_element_type=F32) + bg_ref[...])
    return z * gate


def _s5_prompt_kernel(u_ref, h0_ref, ab_ref, bb_ref, cc_ref, d_ref, wg_ref, bg_ref, o_ref, hT_ref,
                      h_sc, bu_sc, hs_sc, *, ts, n_state):
    t = pl.program_id(1)

    @pl.when(t == 0)
    def _():
        h_sc[...] = h0_ref[0]

    u = u_ref[...]
    bu_sc[...] = jnp.dot(u.astype(BF16), bb_ref[...], preferred_element_type=F32)
    a_re = ab_ref[0:1, :]
    a_im = ab_ref[1:2, :]

    def body(tt, carry):
        h_re, h_im = carry
        b_re = bu_sc[pl.ds(tt, 1), :n_state]
        b_im = bu_sc[pl.ds(tt, 1), n_state:]
        n_re = a_re * h_re - a_im * h_im + b_re
        n_im = a_re * h_im + a_im * h_re + b_im
        hs_sc[pl.ds(tt, 1), :n_state] = n_re
        hs_sc[pl.ds(tt, 1), n_state:] = n_im
        return n_re, n_im

    h_re, h_im = lax.fori_loop(0, ts, body, (h_sc[:, :n_state], h_sc[:, n_state:]), unroll=8)
    h_sc[:, :n_state] = h_re
    h_sc[:, n_state:] = h_im
    o_ref[...] = _s5_tail(hs_sc[...].astype(BF16), u, cc_ref, d_ref, wg_ref, bg_ref)

    @pl.when(t == pl.num_programs(1) - 1)
    def _():
        hT_ref[0] = h_sc[...]


def _s5_prompt(u, h0, ab, bb, cc, d, wg, bg, *, batch, seq_len, ts):
    n, d_ssm = u.shape
    n_state = ab.shape[1]
    nt = seq_len // ts
    assert seq_len % ts == 0
    consts = [ab, bb, cc, d, wg, bg]
    return pl.pallas_call(
        functools.partial(_s5_prompt_kernel, ts=ts, n_state=n_state),
        grid=(batch, nt),
        in_specs=[pl.BlockSpec((ts, d_ssm), lambda b, t: (b * nt + t, 0)),
                  pl.BlockSpec((1, 1, 2 * n_state), lambda b, t: (b, 0, 0))]
        + [_const_spec(c.shape) for c in consts],
        out_specs=[pl.BlockSpec((ts, d_ssm), lambda b, t: (b * nt + t, 0)),
                   pl.BlockSpec((1, 1, 2 * n_state), lambda b, t: (b, 0, 0))],
        out_shape=[jax.ShapeDtypeStruct((n, d_ssm), F32), jax.ShapeDtypeStruct((batch, 1, 2 * n_state), F32)],
        scratch_shapes=[pltpu.VMEM((1, 2 * n_state), F32), pltpu.VMEM((ts, 2 * n_state), F32),
                        pltpu.VMEM((ts, 2 * n_state), F32)],
        compiler_params=_cparams("parallel", "arbitrary"),
        name="s5_prompt",
    )(u, h0, *consts)


def _s5_sample_kernel(u_ref, h0_ref, ab_ref, bb_ref, cc_ref, d_ref, wg_ref, bg_ref, o_ref, hT_ref, *, n_state):
    a_re = ab_ref[0:1, :]
    a_im = ab_ref[1:2, :]
    h_re = h0_ref[:, :n_state]
    h_im = h0_ref[:, n_state:]
    for t in range(u_ref.shape[0]):
        u = u_ref[t]
        bu = jnp.dot(u, bb_ref[...], precision=HIGHEST, preferred_element_type=F32)
        h_re, h_im = (a_re * h_re - a_im * h_im + bu[:, :n_state],
                      a_re * h_im + a_im * h_re + bu[:, n_state:])
        hs = jnp.concatenate([h_re, h_im], axis=1)
        o_ref[t] = _s5_tail(hs.astype(BF16), u, cc_ref, d_ref, wg_ref, bg_ref)
    hT_ref[:, :n_state] = h_re
    hT_ref[:, n_state:] = h_im


def _s5_sample(u, h0, ab, bb_f32, cc, d, wg, bg):
    n_new, n_seq, d_ssm = u.shape
    n_state = ab.shape[1]
    return pl.pallas_call(
        functools.partial(_s5_sample_kernel, n_state=n_state),
        out_shape=[jax.ShapeDtypeStruct((n_new, n_seq, d_ssm), F32), jax.ShapeDtypeStruct((n_seq, 2 * n_state), F32)],
        compiler_params=pltpu.CompilerParams(vmem_limit_bytes=VMEM_LIMIT),
        name="s5_sample",
    )(u, h0, ab, bb_f32, cc, d, wg, bg)


def _post_kernel(h_ref, att_ref, mlp_ref, ssm_ref, pe_ref, gmo_ref, wo_ref, gf_ref, wg_ref, wu_ref, wd_ref,
                 gp_ref, wpg_ref, wpp_ref, o_ref, *, d_att, d_mlp):
    gmo = gmo_ref[...]
    e1 = d_att + d_mlp
    a = (_rms(att_ref[...]) * gmo[:, :d_att]).astype(BF16)
    m = (_rms(mlp_ref[...]) * gmo[:, d_att:e1]).astype(BF16)
    s = (_rms(ssm_ref[...]) * gmo[:, e1:]).astype(BF16)
    h = h_ref[...] + (jnp.dot(a, wo_ref[:d_att, :], preferred_element_type=F32)
                      + jnp.dot(m, wo_ref[d_att:e1, :], preferred_element_type=F32)
                      + jnp.dot(s, wo_ref[e1:, :], preferred_element_type=F32))
    hn = (_rms(h) * gf_ref[...]).astype(BF16)
    ff = jax.nn.silu(jnp.dot(hn, wg_ref[...], preferred_element_type=F32)) * jnp.dot(
        hn, wu_ref[...], preferred_element_type=F32)
    h = h + jnp.dot(ff.astype(BF16), wd_ref[...], preferred_element_type=F32)
    gate = jax.nn.sigmoid(jnp.dot((_rms(h) * gp_ref[...]).astype(BF16), wpg_ref[...], preferred_element_type=F32))
    o_ref[...] = h + jnp.dot(pe_ref[...].astype(BF16), wpp_ref[...], preferred_element_type=F32) * gate


def _post(h, att, mlp, ssm, pe, gmo, wo, gf, wg, wu, wd, gp, wpg, wpp, *, tm):
    n, d_model = h.shape
    assert n % tm == 0
    row = lambda a: pl.BlockSpec((tm, a.shape[1]), lambda i: (i, 0))
    consts = [gmo, wo, gf, wg, wu, wd, gp, wpg, wpp]
    return pl.pallas_call(
        functools.partial(_post_kernel, d_att=att.shape[1], d_mlp=mlp.shape[1]),
        grid=(n // tm,),
        in_specs=[row(h), row(att), row(mlp), row(ssm), row(pe)] + [_const_spec(c.shape) for c in consts],
        out_specs=row(h),
        out_shape=jax.ShapeDtypeStruct((n, d_model), F32),
        compiler_params=_cparams("parallel"),
        name="post",
    )(h, att, mlp, ssm, pe, *consts)


def _block_diag(blocks):
    g, r, c = blocks.shape
    eye = jnp.eye(g, dtype=blocks.dtype)
    return (eye[:, None, :, None] * blocks[:, :, None, :]).reshape(g * r, g * c)


def _s5_params(a_re, a_im, log_dt, b_re, b_im, c_re, c_im, d, w_glu, b_glu):
    dt = jnp.exp(log_dt)[:, None]
    mag = jnp.exp(a_re * dt)
    ab_re = mag * jnp.cos(a_im * dt)
    ab_im = mag * jnp.sin(a_im * dt)
    den = a_re * a_re + a_im * a_im
    coef_re = ((ab_re - 1.0) * a_re + ab_im * a_im) / den
    coef_im = (ab_im * a_re - (ab_re - 1.0) * a_im) / den
    bb_re = coef_re[..., None] * b_re - coef_im[..., None] * b_im
    bb_im = coef_re[..., None] * b_im + coef_im[..., None] * b_re
    ab = jnp.stack([ab_re.reshape(-1), ab_im.reshape(-1)])
    bb = jnp.concatenate([_block_diag(jnp.swapaxes(bb_re, 1, 2)), _block_diag(jnp.swapaxes(bb_im, 1, 2))], axis=1)
    cc = jnp.concatenate([_block_diag(jnp.swapaxes(c_re, 1, 2)), -_block_diag(jnp.swapaxes(c_im, 1, 2))], axis=0)
    return ab, bb, cc.astype(BF16), d[None, :], _block_diag(w_glu).astype(BF16), b_glu[None, :]


def kernel(x_prompt, x_sample, cache_k, cache_v, cache_logf, state_ssm_re, state_ssm_im, page_table, p_prompt, p_sample, g_norm_mix, w_in, b_forget, g_q, g_k, w_spatial, b_spatial, ssm_a_re, ssm_a_im, ssm_log_dt, ssm_b_re, ssm_b_im, ssm_c_re, ssm_c_im, ssm_d, w_glu, b_glu, g_mix_out, w_out, g_norm_ffn, w_ffn_gate, w_ffn_up, w_ffn_down, g_norm_ple, w_ple_gate, w_ple_proj):
    batch, seq_len, d_model = x_prompt.shape
    n_seq, n_new, _ = x_sample.shape
    depth, n_pool, page, n_heads, head_dim = cache_k.shape
    assert head_dim == HEAD_DIM and page == CHUNK and n_seq * n_new == CHUNK
    d_att = n_heads * HEAD_DIM
    n_mlp_heads = w_spatial.shape[1]
    d_mlp = n_mlp_heads * HEAD_DIM
    n_groups, n_st = ssm_a_re.shape[1:]
    d_ssm = n_groups * SSM_CH
    n_state = n_groups * n_st
    n_p = batch * seq_len
    n_s = n_seq * n_new

    tm_in = min(512, seq_len)
    tq = min(512, seq_len)
    ts = min(512, seq_len)
    tm_post = min(256, seq_len)
    pages_per_step = 4

    cache_kt = jnp.transpose(cache_k, (0, 1, 3, 4, 2)).reshape(depth, n_pool, d_att, page)
    cache_vt = jnp.transpose(cache_v, (0, 1, 3, 4, 2)).reshape(depth, n_pool, d_att, page)
    cache_lft = jnp.swapaxes(cache_logf, 2, 3)

    gavg = _block_diag(jnp.full((n_heads, HEAD_DIM, HEAD_DIM), 1.0 / HEAD_DIM, F32)).astype(BF16)
    tril_new = jnp.tril(jnp.ones((n_new, n_new), F32))
    eye_seq = jnp.eye(n_seq, dtype=F32)

    hp = x_prompt.reshape(n_p, d_model)
    hs = x_sample.reshape(n_s, d_model)
    h0_prompt = jnp.zeros((batch, 1, 2 * n_state), F32)
    outs = {name: [] for name in ("kp", "vp", "lfp", "srp", "sip", "ks", "vs", "lfs", "srs", "sis", "mvs")}

    for i in range(depth):
        f0, f1 = 3 * d_att, 3 * d_att + n_heads
        w_q, w_k, w_v, w_f, w_rest = (w_in[i][:, :d_att], w_in[i][:, d_att:2 * d_att], w_in[i][:, 2 * d_att:f0],
                                      w_in[i][:, f0:f1], w_in[i][:, f1:])
        w1_p = jnp.concatenate([w_q, w_rest], axis=1).astype(BF16)
        pad_rows = -(2 * d_att + n_heads) % BF16_ROWS
        w2t_p = jnp.concatenate([w_k.T, w_v.T, w_f.T, jnp.zeros((pad_rows, d_model), F32)], axis=0).astype(BF16)
        w1_s = jnp.concatenate([w_q, w_rest, w_k, w_v, w_f, jnp.zeros((d_model, LANES - n_heads), F32)],
                               axis=1).astype(BF16)
        bf_row = jnp.concatenate([b_forget[i], jnp.zeros((LANES - n_heads,), F32)])[None, :]
        gq = jnp.tile(g_q[i], n_heads)[None, :]
        gk = jnp.tile(g_k[i], n_heads)
        bs_p = jnp.repeat(b_spatial[i].T, HEAD_DIM, axis=1)
        ws_s = jnp.einsum("st,hij->hsitj", eye_seq, w_spatial[i][:, :n_new, :n_new] * tril_new).reshape(
            n_mlp_heads, n_s, n_s)
        bs_s = jnp.tile(jnp.repeat(b_spatial[i][:, :n_new].T, HEAD_DIM, axis=1), (n_seq, 1))
        ab, bb, cc, d_row, wglu_bd, bglu = _s5_params(
            ssm_a_re[i], ssm_a_im[i], ssm_log_dt[i], ssm_b_re[i], ssm_b_im[i], ssm_c_re[i], ssm_c_im[i],
            ssm_d[i], w_glu[i], b_glu[i])
        post_w = (g_mix_out[i][None, :], w_out[i].astype(BF16), g_norm_ffn[i][None, :], w_ffn_gate[i].astype(BF16),
                  w_ffn_up[i].astype(BF16), w_ffn_down[i].astype(BF16), g_norm_ple[i][None, :],
                  w_ple_gate[i].astype(BF16), w_ple_proj[i].astype(BF16))
        dims = dict(d_att=d_att, d_mlp=d_mlp, d_ssm=d_ssm)

        q_bf, kt, vt, kt_bf, vt_bf, lf, mlp, su = _proj_in(
            hp, g_norm_mix[i][None, :], w1_p, w2t_p, b_forget[i][:, None], gq, gk[:, None], gavg, w_spatial[i], bs_p,
            tm=tm_in, seq_len=seq_len, **dims)
        att = _attn_prompt(q_bf, kt_bf, vt_bf, _cumsum_seq(lf), tq=tq)
        ssm, h_t = _s5_prompt(su, h0_prompt, ab, bb.astype(BF16), cc, d_row, wglu_bd, bglu,
                              batch=batch, seq_len=seq_len, ts=ts)
        hp = _post(hp, att, mlp, ssm, p_prompt[i].reshape(n_p, -1), *post_w, tm=tm_post)
        outs["kp"].append(kt)
        outs["vp"].append(vt)
        outs["lfp"].append(lf)
        outs["srp"].append(h_t[:, 0, :n_state].reshape(batch, n_groups, n_st))
        outs["sip"].append(h_t[:, 0, n_state:].reshape(batch, n_groups, n_st))

        q_bf, k, v, lf_t, mlp, su, mv = _proj_in(
            hs, g_norm_mix[i][None, :], w1_s, None, bf_row, gq, gk[None, :], gavg, ws_s, bs_s, tm=n_s, **dims)
        lf_new = jnp.swapaxes(lf_t.reshape(n_heads, n_seq, n_new), 0, 1)
        att = _attn_sample(page_table, q_bf.reshape(n_seq, n_new, d_att), k.reshape(n_seq, n_new, d_att),
                           v.reshape(n_seq, n_new, d_att), lf_new, cache_kt, cache_vt, cache_lft,
                           layer=i, pages_per_step=pages_per_step)
        h0_s = jnp.concatenate([state_ssm_re[i].reshape(n_seq, n_state), state_ssm_im[i].reshape(n_seq, n_state)], axis=1)
        ssm, h_t = _s5_sample(jnp.swapaxes(su.reshape(n_seq, n_new, d_ssm), 0, 1), h0_s, ab, bb, cc, d_row, wglu_bd, bglu)
        ssm = jnp.swapaxes(ssm, 0, 1).reshape(n_s, d_ssm)
        hs = _post(hs, att.reshape(n_s, d_att), mlp, ssm, p_sample[i].reshape(n_s, -1), *post_w, tm=n_s)
        outs["ks"].append(k.reshape(n_seq, n_new, n_heads, HEAD_DIM))
        outs["vs"].append(v.reshape(n_seq, n_new, n_heads, HEAD_DIM))
        outs["lfs"].append(lf_t.T.reshape(n_seq, n_new, n_heads))
        outs["srs"].append(h_t[:, :n_state].reshape(n_seq, n_groups, n_st))
        outs["sis"].append(h_t[:, n_state:].reshape(n_seq, n_groups, n_st))
        outs["mvs"].append(mv.reshape(n_seq, n_new, d_mlp))

    st = {name: jnp.stack(v) for name, v in outs.items()}
    to_heads = lambda a: jnp.transpose(a.reshape(depth, batch, n_heads, HEAD_DIM, seq_len), (0, 1, 4, 2, 3))
    return (hp.reshape(batch, seq_len, d_model), hs.reshape(n_seq, n_new, d_model),
            to_heads(st["kp"]), to_heads(st["vp"]), jnp.swapaxes(st["lfp"], 2, 3), st["srp"], st["sip"],
            st["ks"], st["vs"], st["lfs"], st["srs"], st["sis"], st["mvs"])
```

```python
import functools

import jax
import jax.numpy as jnp
from jax import lax
from jax.experimental import pallas as pl
from jax.experimental.pallas import tpu as pltpu

F32 = jnp.float32
BF16 = jnp.bfloat16

HEAD_DIM = 64
CHUNK = 128
SSM_CH = 16
EPS = 1e-6
LANES = 128
BF16_ROWS = 16
NEG = -1e30
LOG2E = 1.4426950408889634

VMEM_LIMIT = 52 * 1024 * 1024
HIGHEST = lax.Precision.HIGHEST
NT_DIMS = (((1,), (1,)), ((), ()))


def _cparams(*sem):
    return pltpu.CompilerParams(dimension_semantics=sem, vmem_limit_bytes=VMEM_LIMIT)


def _rms(x):
    return x * lax.rsqrt(jnp.mean(x * x, axis=-1, keepdims=True) + EPS)


def _log_sigmoid(x):
    return jnp.minimum(x, 0.0) - jnp.log1p(jnp.exp(-jnp.abs(x)))


def _const_spec(shape):
    return pl.BlockSpec(shape, lambda *_: (0,) * len(shape), pipeline_mode=pl.Buffered(1))


def _split3_bf16(x):
    hi = x.astype(BF16).astype(F32)
    r = x - hi
    mid = r.astype(BF16).astype(F32)
    return hi, mid, (r - mid).astype(BF16).astype(F32)


def _proj_in_kernel(*refs, d_att, d_mlp, d_ssm, tm, prompt):
    if prompt:
        (h_ref, g_ref, w1_ref, w2t_ref, bfr_ref, bfc_ref, gqc_ref, gkr_ref, gkc_ref, gavg_ref, ws_ref, bs_ref,
         qt_ref, kb_ref, aug_ref, tot_ref, kt_ref, vt_ref, vtb_ref, lf_ref, mlp_ref, su_ref) = refs
    else:
        (h_ref, g_ref, w1_ref, bfr_ref, gqr_ref, gkr_ref, gavg_ref, ws_ref, bs_ref,
         q_ref, k_ref, v_ref, lf_ref, mlp_ref, su_ref, mv_ref) = refs
    n_heads = d_att // HEAD_DIM
    xn = (_rms(h_ref[...]) * g_ref[...]).astype(BF16)
    z = jnp.dot(xn, w1_ref[...], preferred_element_type=F32)
    mu = jax.nn.gelu(z[:, d_att:d_att + d_mlp])
    mv = jax.nn.gelu(z[:, d_att + d_mlp:d_att + 2 * d_mlp])
    o = d_att + 2 * d_mlp + d_ssm
    su_ref[...] = z[:, d_att + 2 * d_mlp:o]
    lane = lax.broadcasted_iota(jnp.int32, (tm, LANES), 1)
    lf_tok = jnp.where(lane < n_heads, _log_sigmoid(z[:, o:o + LANES] + bfr_ref[...]), 0.0)

    gavg = gavg_ref[...]

    def head_norm(t, g):
        ms = jnp.dot((t * t).astype(BF16), gavg, preferred_element_type=F32)
        return t * lax.rsqrt(ms + EPS) * g

    def head_norm_t(t, g):
        t3 = t.reshape(n_heads, HEAD_DIM, tm)
        ms = jnp.mean(t3 * t3, axis=1, keepdims=True)
        return (t3 * lax.rsqrt(ms + EPS)).reshape(d_att, tm) * g

    if prompt:
        kb_ref[...] = head_norm(z[:, :d_att], gkr_ref[...]).astype(BF16)
        row = lax.broadcasted_iota(jnp.int32, (tm, LANES), 0)
        c_loc = lf_tok
        s = 1
        while s < tm:
            c_loc = c_loc + jnp.where(row >= s, pltpu.roll(c_loc, s, 0), 0.0)
            s *= 2
        hi, mid, lo = _split3_bf16(-LOG2E * c_loc)
        aug_ref[...] = (hi + pltpu.roll(mid, n_heads, 1) + pltpu.roll(lo, 2 * n_heads, 1)).astype(BF16)

        f = lax.dot_general(w2t_ref[...], xn, NT_DIMS, preferred_element_type=F32)
        qt_ref[0] = (head_norm_t(f[:d_att], gqc_ref[...]) * (LOG2E * HEAD_DIM ** -0.5)).astype(BF16)
        kt_ref[0] = head_norm_t(f[d_att:2 * d_att], gkc_ref[...])
        vt = f[2 * d_att:3 * d_att]
        vt_ref[0] = vt
        vtb_ref[0] = vt.astype(BF16)
        lf_t = _log_sigmoid(f[3 * d_att:3 * d_att + n_heads] + bfc_ref[...])
        lf_ref[0] = lf_t
        tot_ref[0, 0] = jnp.broadcast_to(LOG2E * jnp.sum(lf_t, axis=1, keepdims=True), (n_heads, LANES))
    else:
        q_ref[...] = (head_norm(z[:, :d_att], gqr_ref[...]) * (HEAD_DIM ** -0.5)).astype(BF16)
        k_ref[...] = head_norm(z[:, o + LANES:o + LANES + d_att], gkr_ref[...])
        v_ref[...] = z[:, o + LANES + d_att:]
        lf_ref[...] = lf_tok.T[:n_heads, :]
        mv_ref[...] = mv

    n_mlp_heads = d_mlp // HEAD_DIM
    rows = lax.broadcasted_iota(jnp.int32, (CHUNK, CHUNK), 0)
    cols = lax.broadcasted_iota(jnp.int32, (CHUNK, CHUNK), 1)
    wst = [jnp.where(rows >= cols, ws_ref[h], 0.0).astype(BF16) for h in range(n_mlp_heads)]
    lane_head = lax.broadcasted_iota(jnp.int32, (CHUNK, d_mlp), 1) // HEAD_DIM
    for c in range(tm // CHUNK):
        sl = slice(c * CHUNK, (c + 1) * CHUNK)
        mvc = mv[sl].astype(BF16)
        mix = bs_ref[...]
        for h in range(n_mlp_heads):
            r = jnp.dot(wst[h], mvc, preferred_element_type=F32)
            mix = mix + jnp.where(lane_head == h, r, 0.0)
        mlp_ref[sl, :] = mu[sl] * mix


def _proj_in(h, consts, *, tm, d_att, d_mlp, d_ssm, seq_len=None):
    n, d_model = h.shape
    n_heads = d_att // HEAD_DIM
    prompt = seq_len is not None
    assert n % tm == 0 and tm % CHUNK == 0
    row = lambda width: pl.BlockSpec((tm, width), lambda i: (i, 0))
    sds = jax.ShapeDtypeStruct
    if prompt:
        batch, nt = n // seq_len, seq_len // tm
        assert seq_len % tm == 0
        col = lambda height: pl.BlockSpec((1, height, tm), lambda i: (i // nt, 0, i % nt))
        fm = lambda dt: sds((batch, d_att, seq_len), dt)
        out_shape = [fm(BF16), sds((n, d_att), BF16), sds((n, LANES), BF16), sds((batch, nt, n_heads, LANES), F32),
                     fm(F32), fm(F32), fm(BF16), sds((batch, n_heads, seq_len), F32),
                     sds((n, d_mlp), F32), sds((n, d_ssm), F32)]
        out_specs = [col(d_att), row(d_att), row(LANES),
                     pl.BlockSpec((1, 1, n_heads, LANES), lambda i: (i // nt, i % nt, 0, 0)),
                     col(d_att), col(d_att), col(d_att), col(n_heads), row(d_mlp), row(d_ssm)]
    else:
        out_shape = [sds((n, d_att), BF16), sds((n, d_att), F32), sds((n, d_att), F32), sds((n_heads, n), F32),
                     sds((n, d_mlp), F32), sds((n, d_ssm), F32), sds((n, d_mlp), F32)]
        out_specs = [row(d_att), row(d_att), row(d_att), pl.BlockSpec((n_heads, tm), lambda i: (0, i)),
                     row(d_mlp), row(d_ssm), row(d_mlp)]
    return pl.pallas_call(
        functools.partial(_proj_in_kernel, d_att=d_att, d_mlp=d_mlp, d_ssm=d_ssm, tm=tm, prompt=prompt),
        grid=(n // tm,),
        in_specs=[row(d_model)] + [_const_spec(c.shape) for c in consts],
        out_specs=out_specs,
        out_shape=out_shape,
        compiler_params=_cparams("parallel"),
        name="proj_in",
    )(h, *consts)


def _attn_prompt_kernel(qt_ref, k_ref, aug_ref, vt_ref, tot_ref, o_ref, m_sc, l_sc, acc_sc, *, tq, n_heads):
    hp = pl.program_id(1)
    i = pl.program_id(2)
    qt = qt_ref[0].astype(F32)
    row = lax.broadcasted_iota(jnp.int32, (LANES, tq), 0)
    q_aug = []
    for e in range(2):
        h = 2 * hp + e
        own = (row < HEAD_DIM) if e == 0 else (row >= HEAD_DIM)
        ones_rows = (row == h) | (row == n_heads + h) | (row == 2 * n_heads + h)
        q_aug.append(jnp.concatenate([jnp.where(own, qt, 0.0), jnp.where(ones_rows, 1.0, 0.0)], axis=0).astype(BF16))

    m_sc[...] = jnp.full(m_sc.shape, NEG, F32)
    l_sc[...] = jnp.zeros(l_sc.shape, F32)
    acc_sc[...] = jnp.zeros(acc_sc.shape, F32)

    def step(j, refs, masked):
        k0 = pl.multiple_of(j * tq, tq)
        ka = jnp.concatenate([k_ref[pl.ds(k0, tq), :], aug_ref[pl.ds(k0, tq), :]], axis=1)
        vt = vt_ref[0, :, pl.ds(k0, tq)]
        logits = [jnp.dot(ka, q_aug[e], preferred_element_type=F32) for e in range(2)]
        new_refs = []
        for e in range(2):
            s = logits[e]
            if masked:
                kpos = lax.broadcasted_iota(jnp.int32, (tq, tq), 0)
                qpos = lax.broadcasted_iota(jnp.int32, (tq, tq), 1)
                s = jnp.where(kpos <= qpos, s, NEG)
            ref = jnp.tile(refs[e], (1, tq // LANES))
            m_old = m_sc[e]
            m_new = jnp.maximum(m_old, jnp.max(s, axis=0, keepdims=True) - ref)
            alpha = jnp.exp2(m_old - m_new)
            p = jnp.exp2(s - (m_new + ref))
            l_sc[e] = alpha * l_sc[e] + jnp.sum(p, axis=0, keepdims=True)
            pv = jnp.dot(vt[e * HEAD_DIM:(e + 1) * HEAD_DIM, :], p.astype(BF16), preferred_element_type=F32)
            acc_sc[e] = alpha * acc_sc[e] + pv
            m_sc[e] = m_new
            new_refs.append(refs[e] + tot_ref[0, j, pl.ds(2 * hp + e, 1), :])
        return tuple(new_refs)

    zero_ref = jnp.zeros((1, LANES), F32)
    refs = lax.fori_loop(0, i, lambda j, refs: step(j, refs, False), (zero_ref, zero_ref))
    step(i, refs, True)
    out_t = jnp.concatenate([acc_sc[0] / l_sc[0], acc_sc[1] / l_sc[1]], axis=0)
    o_ref[...] = out_t.T


def _attn_prompt(qt_bf, k_bf, aug_bf, vt_bf, tot, *, tq):
    batch, d_att, seq_len = qt_bf.shape
    n_heads = d_att // HEAD_DIM
    n_pairs = d_att // LANES
    nq = seq_len // tq
    assert seq_len % tq == 0 and tq % LANES == 0 and tot.shape == (batch, nq, n_heads, LANES)
    return pl.pallas_call(
        functools.partial(_attn_prompt_kernel, tq=tq, n_heads=n_heads),
        grid=(batch, n_pairs, nq),
        in_specs=[pl.BlockSpec((1, LANES, tq), lambda b, hp, i: (b, hp, i)),
                  pl.BlockSpec((seq_len, LANES), lambda b, hp, i: (b, hp)),
                  pl.BlockSpec((seq_len, LANES), lambda b, hp, i: (b, 0)),
                  pl.BlockSpec((1, LANES, seq_len), lambda b, hp, i: (b, hp, 0)),
                  pl.BlockSpec((1, nq, n_heads, LANES), lambda b, hp, i: (b, 0, 0, 0))],
        out_specs=pl.BlockSpec((tq, LANES), lambda b, hp, i: (b * nq + i, hp)),
        out_shape=jax.ShapeDtypeStruct((batch * seq_len, d_att), F32),
        scratch_shapes=[pltpu.VMEM((2, 1, tq), F32), pltpu.VMEM((2, 1, tq), F32), pltpu.VMEM((2, HEAD_DIM, tq), F32)],
        compiler_params=_cparams("parallel", "parallel", "parallel"),
        name="attn_prompt",
    )(qt_bf, k_bf, aug_bf, vt_bf, tot)


def _attn_sample_kernel(pt_ref, q_ref, kn_ref, vn_ref, lfn_ref, *refs, pages_per_step, n_new):
    del pt_ref
    pg = pages_per_step
    k_refs, v_refs, lf_refs = refs[:pg], refs[pg:2 * pg], refs[2 * pg:3 * pg]
    o_ref, qbd_sc, m_sc, l_sc, acc_sc, coff_sc = refs[3 * pg:]
    g = pl.program_id(1)
    n_rows, d_att = qbd_sc.shape
    n_heads = d_att // HEAD_DIM
    row_head = lax.broadcasted_iota(jnp.int32, (n_rows, d_att), 0) % n_heads
    lane_head = lax.broadcasted_iota(jnp.int32, (n_rows, d_att), 1) // HEAD_DIM
    own = row_head == lane_head

    @pl.when(g == 0)
    def _init():
        qf = q_ref[0].astype(F32)
        rows = jnp.concatenate([jnp.broadcast_to(qf[i:i + 1], (n_heads, d_att)) for i in range(n_new)], axis=0)
        qbd_sc[...] = jnp.where(own, rows, 0.0).astype(BF16)
        m_sc[...] = jnp.full(m_sc.shape, NEG, F32)
        l_sc[...] = jnp.zeros(l_sc.shape, F32)
        acc_sc[...] = jnp.zeros(acc_sc.shape, F32)
        coff_sc[...] = jnp.zeros(coff_sc.shape, F32)

    lf = jnp.concatenate([lf_refs[p][0, 0] for p in range(pg)], axis=0)
    tri = (lax.broadcasted_iota(jnp.int32, (CHUNK, CHUNK), 0)
           <= lax.broadcasted_iota(jnp.int32, (CHUNK, CHUNK), 1)).astype(BF16)
    c_in = sum(jnp.dot(t.astype(BF16), tri, preferred_element_type=F32) for t in _split3_bf16(lf))
    tot = jnp.broadcast_to(c_in[:, CHUNK - 1:CHUNK], c_in.shape)
    page = lax.broadcasted_iota(jnp.int32, c_in.shape, 0) // n_heads
    incl = tot
    s = 1
    while s < pg:
        incl = incl + jnp.where(page >= s, pltpu.roll(incl, s * n_heads, 0), 0.0)
        s *= 2
    coff = coff_sc[...]
    ck = c_in + (incl - tot) + jnp.concatenate([coff] * pg, axis=0)
    coff_sc[...] = coff + incl[(pg - 1) * n_heads:, :]

    qbd = qbd_sc[...]
    s_parts = []
    for p in range(pg):
        ck_p = jnp.concatenate([ck[p * n_heads:(p + 1) * n_heads]] * n_new, axis=0)
        s_parts.append(jnp.dot(qbd, k_refs[p][0, 0].astype(BF16), preferred_element_type=F32) - ck_p)
    s_all = jnp.concatenate(s_parts, axis=1)
    m_old = m_sc[...]
    m_new = jnp.maximum(m_old, jnp.max(s_all, axis=1, keepdims=True))
    alpha = jnp.exp(m_old - m_new)
    p_all = jnp.exp(s_all - m_new)
    l_sc[...] = alpha * l_sc[...] + jnp.sum(p_all, axis=1, keepdims=True)
    pv = jnp.zeros(acc_sc.shape, F32)
    for p in range(pg):
        pv = pv + lax.dot_general(p_all[:, p * CHUNK:(p + 1) * CHUNK].astype(BF16), v_refs[p][0, 0].astype(BF16),
                                  NT_DIMS, preferred_element_type=F32)
    acc_sc[...] = alpha * acc_sc[...] + pv
    m_sc[...] = m_new

    @pl.when(g == pl.num_programs(1) - 1)
    def _finish():
        qf = qbd_sc[...].astype(F32)
        row_query = lax.broadcasted_iota(jnp.int32, (n_rows, 1), 0) // n_heads
        lfn = jnp.concatenate([lfn_ref[0]] * n_new, axis=0)
        c = jnp.concatenate([coff_sc[...]] * n_new, axis=0)[:, 0:1]
        m, l, acc = m_sc[...], l_sc[...], acc_sc[...]
        for j in range(n_new):
            c = c + lfn[:, j:j + 1]
            s = jnp.sum(qf * kn_ref[0, j:j + 1, :], axis=1, keepdims=True) - c
            s = jnp.where(row_query >= j, s, NEG)
            m_new = jnp.maximum(m, s)
            alpha = jnp.exp(m - m_new)
            pj = jnp.exp(s - m_new)
            l = alpha * l + pj
            acc = alpha * acc + pj * vn_ref[0, j:j + 1, :]
            m = m_new
        out = jnp.where(own, acc / l, 0.0)
        for i in range(n_new):
            o_ref[0, i:i + 1, :] = jnp.sum(out[i * n_heads:(i + 1) * n_heads], axis=0, keepdims=True)


def _attn_sample(page_table, q_bf, k_new, v_new, lf_new, cache_kt, cache_vt, cache_lft, *, layer, pages_per_step):
    n_seq, n_new, d_att = q_bf.shape
    n_heads = d_att // HEAD_DIM
    n_pages = page_table.shape[1]
    pg = pages_per_step
    assert n_pages % pg == 0 and cache_kt.shape[2:] == (d_att, CHUNK) and cache_lft.shape[2:] == (n_heads, CHUNK)
    n_rows = n_new * n_heads

    def page_spec(height, p):
        return pl.BlockSpec((1, 1, height, CHUNK), lambda s, g, pt: (layer, pt[s, g * pg + p], 0, 0))

    seq_spec = lambda a: pl.BlockSpec((1,) + a.shape[1:], lambda s, g, pt: (s, 0, 0))
    grid_spec = pltpu.PrefetchScalarGridSpec(
        num_scalar_prefetch=1,
        grid=(n_seq, n_pages // pg),
        in_specs=[seq_spec(q_bf), seq_spec(k_new), seq_spec(v_new), seq_spec(lf_new)]
        + [page_spec(d_att, p) for p in range(pg)]
        + [page_spec(d_att, p) for p in range(pg)]
        + [page_spec(n_heads, p) for p in range(pg)],
        out_specs=pl.BlockSpec((1, n_new, d_att), lambda s, g, pt: (s, 0, 0)),
        scratch_shapes=[pltpu.VMEM((n_rows, d_att), BF16), pltpu.VMEM((n_rows, 1), F32), pltpu.VMEM((n_rows, 1), F32),
                        pltpu.VMEM((n_rows, d_att), F32), pltpu.VMEM((n_heads, CHUNK), F32)],
    )
    return pl.pallas_call(
        functools.partial(_attn_sample_kernel, pages_per_step=pg, n_new=n_new),
        grid_spec=grid_spec,
        out_shape=jax.ShapeDtypeStruct((n_seq, n_new, d_att), F32),
        compiler_params=_cparams("parallel", "arbitrary"),
        name="attn_sample",
    )(page_table, q_bf, k_new, v_new, lf_new, *([cache_kt] * pg), *([cache_vt] * pg), *([cache_lft] * pg))


def _s5_tail(hs_bf, u, cc_ref, d_ref, wg_ref, bg_ref):
    y = jnp.dot(hs_bf, cc_ref[...], preferred_element_type=F32) + d_ref[...] * u
    z = jax.nn.gelu(y)
    gate = jax.nn.sigmoid(jnp.dot(z.astype(BF16), wg_ref[...], preferred_element_type=F32) + bg_ref[...])
    return z * gate


def _s5_prompt_kernel(u_ref, h0_ref, ab_ref, bb_ref, cc_ref, d_ref, wg_ref, bg_ref, o_ref, hT_ref,
                      h_sc, bu_sc, hs_sc, *, ts, n_state):
    t = pl.program_id(1)

    @pl.when(t == 0)
    def _():
        h_sc[...] = h0_ref[0]

    u = u_ref[...]
    bu_sc[...] = jnp.dot(u.astype(BF16), bb_ref[...], preferred_element_type=F32)
    a_re = ab_ref[0:1, :]
    a_im = ab_ref[1:2, :]

    def body(tt, carry):
        h_re, h_im = carry
        b_re = bu_sc[pl.ds(tt, 1), :n_state]
        b_im = bu_sc[pl.ds(tt, 1), n_state:]
        n_re = a_re * h_re - a_im * h_im + b_re
        n_im = a_re * h_im + a_im * h_re + b_im
        hs_sc[pl.ds(tt, 1), :n_state] = n_re
        hs_sc[pl.ds(tt, 1), n_state:] = n_im
        return n_re, n_im

    h_re, h_im = lax.fori_loop(0, ts, body, (h_sc[:, :n_state], h_sc[:, n_state:]), unroll=8)
    h_sc[:, :n_state] = h_re
    h_sc[:, n_state:] = h_im
    o_ref[...] = _s5_tail(hs_sc[...].astype(BF16), u, cc_ref, d_ref, wg_ref, bg_ref)

    @pl.when(t == pl.num_programs(1) - 1)
    def _():
        hT_ref[0] = h_sc[...]


def _s5_prompt(u, h0, ab, bb, cc, d, wg, bg, *, batch, seq_len, ts):
    n, d_ssm = u.shape
    n_state = ab.shape[1]
    nt = seq_len // ts
    assert seq_len % ts == 0
    consts = [ab, bb, cc, d, wg, bg]
    return pl.pallas_call(
        functools.partial(_s5_prompt_kernel, ts=ts, n_state=n_state),
        grid=(batch, nt),
        in_specs=[pl.BlockSpec((ts, d_ssm), lambda b, t: (b * nt + t, 0)),
                  pl.BlockSpec((1, 1, 2 * n_state), lambda b, t: (b, 0, 0))]
        + [_const_spec(c.shape) for c in consts],
        out_specs=[pl.BlockSpec((ts, d_ssm), lambda b, t: (b * nt + t, 0)),
                   pl.BlockSpec((1, 1, 2 * n_state), lambda b, t: (b, 0, 0))],
        out_shape=[jax.ShapeDtypeStruct((n, d_ssm), F32), jax.ShapeDtypeStruct((batch, 1, 2 * n_state), F32)],
        scratch_shapes=[pltpu.VMEM((1, 2 * n_state), F32), pltpu.VMEM((ts, 2 * n_state), F32),
                        pltpu.VMEM((ts, 2 * n_state), F32)],
        compiler_params=_cparams("parallel", "arbitrary"),
        name="s5_prompt",
    )(u, h0, *consts)


def _s5_sample_kernel(u_ref, h0_ref, ab_ref, bb_ref, cc_ref, d_ref, wg_ref, bg_ref, o_ref, hT_ref, *, n_state):
    a_re = ab_ref[0:1, :]
    a_im = ab_ref[1:2, :]
    h_re = h0_ref[:, :n_state]
    h_im = h0_ref[:, n_state:]
    for t in range(u_ref.shape[0]):
        u = u_ref[t]
        bu = jnp.dot(u, bb_ref[...], precision=HIGHEST, preferred_element_type=F32)
        h_re, h_im = (a_re * h_re - a_im * h_im + bu[:, :n_state],
                      a_re * h_im + a_im * h_re + bu[:, n_state:])
        hs = jnp.concatenate([h_re, h_im], axis=1)
        o_ref[t] = _s5_tail(hs.astype(BF16), u, cc_ref, d_ref, wg_ref, bg_ref)
    hT_ref[:, :n_state] = h_re
    hT_ref[:, n_state:] = h_im


def _s5_sample(u, h0, ab, bb_f32, cc, d, wg, bg):
    n_new, n_seq, d_ssm = u.shape
    n_state = ab.shape[1]
    return pl.pallas_call(
        functools.partial(_s5_sample_kernel, n_state=n_state),
        out_shape=[jax.ShapeDtypeStruct((n_new, n_seq, d_ssm), F32), jax.ShapeDtypeStruct((n_seq, 2 * n_state), F32)],
        compiler_params=pltpu.CompilerParams(vmem_limit_bytes=VMEM_LIMIT),
        name="s5_sample",
    )(u, h0, ab, bb_f32, cc, d, wg, bg)


def _post_kernel(h_ref, att_ref, mlp_ref, ssm_ref, pe_ref, gmo_ref, wo_ref, gf_ref, wg_ref, wu_ref, wd_ref,
                 gp_ref, wpg_ref, wpp_ref, o_ref, *, d_att, d_mlp):
    gmo = gmo_ref[...]
    e1 = d_att + d_mlp
    a = (_rms(att_ref[...]) * gmo[:, :d_att]).astype(BF16)
    m = (_rms(mlp_ref[...]) * gmo[:, d_att:e1]).astype(BF16)
    s = (_rms(ssm_ref[...]) * gmo[:, e1:]).astype(BF16)
    h = h_ref[...] + (jnp.dot(a, wo_ref[:d_att, :], preferred_element_type=F32)
                      + jnp.dot(m, wo_ref[d_att:e1, :], preferred_element_type=F32)
                      + jnp.dot(s, wo_ref[e1:, :], preferred_element_type=F32))
    hn = (_rms(h) * gf_ref[...]).astype(BF16)
    ff = jax.nn.silu(jnp.dot(hn, wg_ref[...], preferred_element_type=F32)) * jnp.dot(
        hn, wu_ref[...], preferred_element_type=F32)
    h = h + jnp.dot(ff.astype(BF16), wd_ref[...], preferred_element_type=F32)
    gate = jax.nn.sigmoid(jnp.dot((_rms(h) * gp_ref[...]).astype(BF16), wpg_ref[...], preferred_element_type=F32))
    o_ref[...] = h + jnp.dot(pe_ref[...].astype(BF16), wpp_ref[...], preferred_element_type=F32) * gate


def _post(h, att, mlp, ssm, pe, gmo, wo, gf, wg, wu, wd, gp, wpg, wpp, *, tm):
    n, d_model = h.shape
    assert n % tm == 0
    row = lambda a: pl.BlockSpec((tm, a.shape[1]), lambda i: (i, 0))
    consts = [gmo, wo, gf, wg, wu, wd, gp, wpg, wpp]
    return pl.pallas_call(
        functools.partial(_post_kernel, d_att=att.shape[1], d_mlp=mlp.shape[1]),
        grid=(n // tm,),
        in_specs=[row(h), row(att), row(mlp), row(ssm), row(pe)] + [_const_spec(c.shape) for c in consts],
        out_specs=row(h),
        out_shape=jax.ShapeDtypeStruct((n, d_model), F32),
        compiler_params=_cparams("parallel"),
        name="post",
    )(h, att, mlp, ssm, pe, *consts)


def _block_diag(blocks):
    g, r, c = blocks.shape
    eye = jnp.eye(g, dtype=blocks.dtype)
    return (eye[:, None, :, None] * blocks[:, :, None, :]).reshape(g * r, g * c)


def _s5_params(a_re, a_im, log_dt, b_re, b_im, c_re, c_im, d, w_glu, b_glu):
    dt = jnp.exp(log_dt)[:, None]
    mag = jnp.exp(a_re * dt)
    ab_re = mag * jnp.cos(a_im * dt)
    ab_im = mag * jnp.sin(a_im * dt)
    den = a_re * a_re + a_im * a_im
    coef_re = ((ab_re - 1.0) * a_re + ab_im * a_im) / den
    coef_im = (ab_im * a_re - (ab_re - 1.0) * a_im) / den
    bb_re = coef_re[..., None] * b_re - coef_im[..., None] * b_im
    bb_im = coef_re[..., None] * b_im + coef_im[..., None] * b_re
    ab = jnp.stack([ab_re.reshape(-1), ab_im.reshape(-1)])
    bb = jnp.concatenate([_block_diag(jnp.swapaxes(bb_re, 1, 2)), _block_diag(jnp.swapaxes(bb_im, 1, 2))], axis=1)
    cc = jnp.concatenate([_block_diag(jnp.swapaxes(c_re, 1, 2)), -_block_diag(jnp.swapaxes(c_im, 1, 2))], axis=0)
    return ab, bb, cc.astype(BF16), d[None, :], _block_diag(w_glu).astype(BF16), b_glu[None, :]


def kernel(x_prompt, x_sample, cache_k, cache_v, cache_logf, state_ssm_re, state_ssm_im, page_table, p_prompt, p_sample, g_norm_mix, w_in, b_forget, g_q, g_k, w_spatial, b_spatial, ssm_a_re, ssm_a_im, ssm_log_dt, ssm_b_re, ssm_b_im, ssm_c_re, ssm_c_im, ssm_d, w_glu, b_glu, g_mix_out, w_out, g_norm_ffn, w_ffn_gate, w_ffn_up, w_ffn_down, g_norm_ple, w_ple_gate, w_ple_proj):
    batch, seq_len, d_model = x_prompt.shape
    n_seq, n_new, _ = x_sample.shape
    depth, n_pool, page, n_heads, head_dim = cache_k.shape
    assert head_dim == HEAD_DIM and page == CHUNK and n_seq * n_new == CHUNK
    d_att = n_heads * HEAD_DIM
    n_mlp_heads = w_spatial.shape[1]
    d_mlp = n_mlp_heads * HEAD_DIM
    n_groups, n_st = ssm_a_re.shape[1:]
    d_ssm = n_groups * SSM_CH
    n_state = n_groups * n_st
    n_p = batch * seq_len
    n_s = n_seq * n_new

    tq = min(512, seq_len)
    ts = min(512, seq_len)
    tm_post = min(256, seq_len)
    pages_per_step = min(16, page_table.shape[1])

    cache_kt = jnp.transpose(cache_k, (0, 1, 3, 4, 2)).reshape(depth, n_pool, d_att, page)
    cache_vt = jnp.transpose(cache_v, (0, 1, 3, 4, 2)).reshape(depth, n_pool, d_att, page)
    cache_lft = jnp.swapaxes(cache_logf, 2, 3)

    gavg = _block_diag(jnp.full((n_heads, HEAD_DIM, HEAD_DIM), 1.0 / HEAD_DIM, F32)).astype(BF16)
    tril_new = jnp.tril(jnp.ones((n_new, n_new), F32))
    eye_seq = jnp.eye(n_seq, dtype=F32)

    hp = x_prompt.reshape(n_p, d_model)
    hs = x_sample.reshape(n_s, d_model)
    h0_prompt = jnp.zeros((batch, 1, 2 * n_state), F32)
    outs = {name: [] for name in ("kp", "vp", "lfp", "srp", "sip", "ks", "vs", "lfs", "srs", "sis", "mvs")}

    for i in range(depth):
        f0, f1 = 3 * d_att, 3 * d_att + n_heads
        w_q, w_k, w_v, w_f, w_rest = (w_in[i][:, :d_att], w_in[i][:, d_att:2 * d_att], w_in[i][:, 2 * d_att:f0],
                                      w_in[i][:, f0:f1], w_in[i][:, f1:])
        w_f_pad = jnp.concatenate([w_f, jnp.zeros((d_model, LANES - n_heads), F32)], axis=1)
        w1_p = jnp.concatenate([w_k, w_rest, w_f_pad], axis=1).astype(BF16)
        pad_rows = -(3 * d_att + n_heads) % BF16_ROWS
        w2t_p = jnp.concatenate([w_q.T, w_k.T, w_v.T, w_f.T, jnp.zeros((pad_rows, d_model), F32)], axis=0).astype(BF16)
        w1_s = jnp.concatenate([w_q, w_rest, w_f_pad, w_k, w_v], axis=1).astype(BF16)
        bf_row = jnp.concatenate([b_forget[i], jnp.zeros((LANES - n_heads,), F32)])[None, :]
        gq = jnp.tile(g_q[i], n_heads)
        gk = jnp.tile(g_k[i], n_heads)
        bs_p = jnp.repeat(b_spatial[i].T, HEAD_DIM, axis=1)
        ws_s = jnp.einsum("st,hij->hsitj", eye_seq, w_spatial[i][:, :n_new, :n_new] * tril_new).reshape(
            n_mlp_heads, n_s, n_s)
        bs_s = jnp.tile(jnp.repeat(b_spatial[i][:, :n_new].T, HEAD_DIM, axis=1), (n_seq, 1))
        ab, bb, cc, d_row, wglu_bd, bglu = _s5_params(
            ssm_a_re[i], ssm_a_im[i], ssm_log_dt[i], ssm_b_re[i], ssm_b_im[i], ssm_c_re[i], ssm_c_im[i],
            ssm_d[i], w_glu[i], b_glu[i])
        post_w = (g_mix_out[i][None, :], w_out[i].astype(BF16), g_norm_ffn[i][None, :], w_ffn_gate[i].astype(BF16),
                  w_ffn_up[i].astype(BF16), w_ffn_down[i].astype(BF16), g_norm_ple[i][None, :],
                  w_ple_gate[i].astype(BF16), w_ple_proj[i].astype(BF16))
        dims = dict(d_att=d_att, d_mlp=d_mlp, d_ssm=d_ssm)
        g_mix = g_norm_mix[i][None, :]

        qt_bf, k_bf, aug_bf, tot, kt, vt, vt_bf, lf, mlp, su = _proj_in(
            hp, [g_mix, w1_p, w2t_p, bf_row, b_forget[i][:, None], gq[:, None], gk[None, :], gk[:, None], gavg,
                 w_spatial[i], bs_p], tm=tq, seq_len=seq_len, **dims)
        att = _attn_prompt(qt_bf, k_bf, aug_bf, vt_bf, tot, tq=tq)
        ssm, h_t = _s5_prompt(su, h0_prompt, ab, bb.astype(BF16), cc, d_row, wglu_bd, bglu,
                              batch=batch, seq_len=seq_len, ts=ts)
        hp = _post(hp, att, mlp, ssm, p_prompt[i].reshape(n_p, -1), *post_w, tm=tm_post)
        outs["kp"].append(kt)
        outs["vp"].append(vt)
        outs["lfp"].append(lf)
        outs["srp"].append(h_t[:, 0, :n_state].reshape(batch, n_groups, n_st))
        outs["sip"].append(h_t[:, 0, n_state:].reshape(batch, n_groups, n_st))

        q_bf, k, v, lf_t, mlp, su, mv = _proj_in(
            hs, [g_mix, w1_s, bf_row, gq[None, :], gk[None, :], gavg, ws_s, bs_s], tm=n_s, **dims)
        lf_new = jnp.swapaxes(lf_t.reshape(n_heads, n_seq, n_new), 0, 1)
        att = _attn_sample(page_table, q_bf.reshape(n_seq, n_new, d_att), k.reshape(n_seq, n_new, d_att),
                           v.reshape(n_seq, n_new, d_att), lf_new, cache_kt, cache_vt, cache_lft,
                           layer=i, pages_per_step=pages_per_step)
        h0_s = jnp.concatenate([state_ssm_re[i].reshape(n_seq, n_state), state_ssm_im[i].reshape(n_seq, n_state)], axis=1)
        ssm, h_t = _s5_sample(jnp.swapaxes(su.reshape(n_seq, n_new, d_ssm), 0, 1), h0_s, ab, bb, cc, d_row, wglu_bd, bglu)
        ssm = jnp.swapaxes(ssm, 0, 1).reshape(n_s, d_ssm)
        hs = _post(hs, att.reshape(n_s, d_att), mlp, ssm, p_sample[i].reshape(n_s, -1), *post_w, tm=n_s)
        outs["ks"].append(k.reshape(n_seq, n_new, n_heads, HEAD_DIM))
        outs["vs"].append(v.reshape(n_seq, n_new, n_heads, HEAD_DIM))
        outs["lfs"].append(lf_t.T.reshape(n_seq, n_new, n_heads))
        outs["srs"].append(h_t[:, :n_state].reshape(n_seq, n_groups, n_st))
        outs["sis"].append(h_t[:, n_state:].reshape(n_seq, n_groups, n_st))
        outs["mvs"].append(mv.reshape(n_seq, n_new, d_mlp))

    st = {name: jnp.stack(v) for name, v in outs.items()}
    to_heads = lambda a: jnp.transpose(a.reshape(depth, batch, n_heads, HEAD_DIM, seq_len), (0, 1, 4, 2, 3))
    return (hp.reshape(batch, seq_len, d_model), hs.reshape(n_seq, n_new, d_model),
            to_heads(st["kp"]), to_heads(st["vp"]), jnp.swapaxes(st["lfp"], 2, 3), st["srp"], st["sip"],
            st["ks"], st["vs"], st["lfs"], st["srs"], st["sis"], st["mvs"])
```

```python
import functools

import jax
import jax.numpy as jnp
from jax import lax
from jax.experimental import pallas as pl
from jax.experimental.pallas import tpu as pltpu

F32 = jnp.float32
BF16 = jnp.bfloat16

HEAD_DIM = 64
CHUNK = 128
SSM_CH = 16
EPS = 1e-6
LANES = 128
BF16_ROWS = 16
NEG = -1e30
LOG2E = 1.4426950408889634

VMEM_LIMIT = 52 * 1024 * 1024
HIGHEST = lax.Precision.HIGHEST
NT_DIMS = (((1,), (1,)), ((), ()))


def _cparams(*sem):
    return pltpu.CompilerParams(dimension_semantics=sem, vmem_limit_bytes=VMEM_LIMIT)


def _rms(x):
    return x * lax.rsqrt(jnp.mean(x * x, axis=-1, keepdims=True) + EPS)


def _log_sigmoid(x):
    return jnp.minimum(x, 0.0) - jnp.log1p(jnp.exp(-jnp.abs(x)))


def _const_spec(shape):
    return pl.BlockSpec(shape, lambda *_: (0,) * len(shape), pipeline_mode=pl.Buffered(1))


def _split3_bf16(x):
    hi = x.astype(BF16).astype(F32)
    r = x - hi
    mid = r.astype(BF16).astype(F32)
    return hi, mid, (r - mid).astype(BF16).astype(F32)


def _proj_in_kernel(*refs, d_att, d_mlp, d_ssm, tm, prompt):
    if prompt:
        (h_ref, g_ref, w1_ref, w2t_ref, bfr_ref, bfc_ref, gqc_ref, gkr_ref, gkc_ref, gavg_ref, ws_ref, bs_ref,
         qt_ref, kb_ref, aug_ref, tot_ref, kt_ref, vt_ref, vtb_ref, lf_ref, mlp_ref, su_ref) = refs
    else:
        (h_ref, g_ref, w1_ref, bfr_ref, gqr_ref, gkr_ref, gavg_ref, ws_ref, bs_ref,
         q_ref, k_ref, v_ref, lf_ref, mlp_ref, su_ref, mv_ref) = refs
    n_heads = d_att // HEAD_DIM
    xn = (_rms(h_ref[...]) * g_ref[...]).astype(BF16)
    z = jnp.dot(xn, w1_ref[...], preferred_element_type=F32)
    mu = jax.nn.gelu(z[:, d_att:d_att + d_mlp])
    mv = jax.nn.gelu(z[:, d_att + d_mlp:d_att + 2 * d_mlp])
    o = d_att + 2 * d_mlp + d_ssm
    su_ref[...] = z[:, d_att + 2 * d_mlp:o]
    lane = lax.broadcasted_iota(jnp.int32, (tm, LANES), 1)
    lf_tok = jnp.where(lane < n_heads, _log_sigmoid(z[:, o:o + LANES] + bfr_ref[...]), 0.0)

    gavg = gavg_ref[...]

    def head_norm(t, g):
        ms = jnp.dot((t * t).astype(BF16), gavg, preferred_element_type=F32)
        return t * lax.rsqrt(ms + EPS) * g

    def head_norm_t(t, g):
        t3 = t.reshape(n_heads, HEAD_DIM, tm)
        ms = jnp.mean(t3 * t3, axis=1, keepdims=True)
        return (t3 * lax.rsqrt(ms + EPS)).reshape(d_att, tm) * g

    if prompt:
        kb_ref[...] = head_norm(z[:, :d_att], gkr_ref[...]).astype(BF16)
        row = lax.broadcasted_iota(jnp.int32, (tm, LANES), 0)
        c_loc = lf_tok
        s = 1
        while s < tm:
            c_loc = c_loc + jnp.where(row >= s, pltpu.roll(c_loc, s, 0), 0.0)
            s *= 2
        hi, mid, lo = _split3_bf16(-LOG2E * c_loc)
        aug_ref[...] = (hi + pltpu.roll(mid, n_heads, 1) + pltpu.roll(lo, 2 * n_heads, 1)).astype(BF16)

        f = lax.dot_general(w2t_ref[...], xn, NT_DIMS, preferred_element_type=F32)
        qt_ref[0] = (head_norm_t(f[:d_att], gqc_ref[...]) * (LOG2E * HEAD_DIM ** -0.5)).astype(BF16)
        kt_ref[0] = head_norm_t(f[d_att:2 * d_att], gkc_ref[...])
        vt = f[2 * d_att:3 * d_att]
        vt_ref[0] = vt
        vtb_ref[0] = vt.astype(BF16)
        lf_t = _log_sigmoid(f[3 * d_att:3 * d_att + n_heads] + bfc_ref[...])
        lf_ref[0] = lf_t
        tot_ref[0, 0] = jnp.broadcast_to(LOG2E * jnp.sum(lf_t, axis=1, keepdims=True), (n_heads, LANES))
    else:
        q_ref[...] = (head_norm(z[:, :d_att], gqr_ref[...]) * (HEAD_DIM ** -0.5)).astype(BF16)
        k_ref[...] = head_norm(z[:, o + LANES:o + LANES + d_att], gkr_ref[...])
        v_ref[...] = z[:, o + LANES + d_att:]
        lf_ref[...] = lf_tok.T[:n_heads, :]
        mv_ref[...] = mv

    n_mlp_heads = d_mlp // HEAD_DIM
    rows = lax.broadcasted_iota(jnp.int32, (CHUNK, CHUNK), 0)
    cols = lax.broadcasted_iota(jnp.int32, (CHUNK, CHUNK), 1)
    wst = [jnp.where(rows >= cols, ws_ref[h], 0.0).astype(BF16) for h in range(n_mlp_heads)]
    lane_head = lax.broadcasted_iota(jnp.int32, (CHUNK, d_mlp), 1) // HEAD_DIM
    for c in range(tm // CHUNK):
        sl = slice(c * CHUNK, (c + 1) * CHUNK)
        mvc = mv[sl].astype(BF16)
        mix = bs_ref[...]
        for h in range(n_mlp_heads):
            r = jnp.dot(wst[h], mvc, preferred_element_type=F32)
            mix = mix + jnp.where(lane_head == h, r, 0.0)
        mlp_ref[sl, :] = mu[sl] * mix


def _proj_in(h, consts, *, tm, d_att, d_mlp, d_ssm, seq_len=None):
    n, d_model = h.shape
    n_heads = d_att // HEAD_DIM
    prompt = seq_len is not None
    assert n % tm == 0 and tm % CHUNK == 0
    row = lambda width: pl.BlockSpec((tm, width), lambda i: (i, 0))
    sds = jax.ShapeDtypeStruct
    if prompt:
        batch, nt = n // seq_len, seq_len // tm
        assert seq_len % tm == 0
        col = lambda height: pl.BlockSpec((1, height, tm), lambda i: (i // nt, 0, i % nt))
        fm = lambda dt: sds((batch, d_att, seq_len), dt)
        out_shape = [fm(BF16), sds((n, d_att), BF16), sds((n, LANES), BF16), sds((batch, nt, n_heads, LANES), F32),
                     fm(F32), fm(F32), fm(BF16), sds((batch, n_heads, seq_len), F32),
                     sds((n, d_mlp), F32), sds((n, d_ssm), F32)]
        out_specs = [col(d_att), row(d_att), row(LANES),
                     pl.BlockSpec((1, 1, n_heads, LANES), lambda i: (i // nt, i % nt, 0, 0)),
                     col(d_att), col(d_att), col(d_att), col(n_heads), row(d_mlp), row(d_ssm)]
    else:
        out_shape = [sds((n, d_att), BF16), sds((n, d_att), F32), sds((n, d_att), F32), sds((n_heads, n), F32),
                     sds((n, d_mlp), F32), sds((n, d_ssm), F32), sds((n, d_mlp), F32)]
        out_specs = [row(d_att), row(d_att), row(d_att), pl.BlockSpec((n_heads, tm), lambda i: (0, i)),
                     row(d_mlp), row(d_ssm), row(d_mlp)]
    return pl.pallas_call(
        functools.partial(_proj_in_kernel, d_att=d_att, d_mlp=d_mlp, d_ssm=d_ssm, tm=tm, prompt=prompt),
        grid=(n // tm,),
        in_specs=[row(d_model)] + [_const_spec(c.shape) for c in consts],
        out_specs=out_specs,
        out_shape=out_shape,
        compiler_params=_cparams("parallel"),
        name="proj_in",
    )(h, *consts)


def _attn_prompt_kernel(qt_ref, k_ref, aug_ref, vt_ref, tot_ref, o_ref, m_sc, l_sc, acc_sc, s_sc, *, tq, n_heads):
    hp = pl.program_id(1)
    i = pl.program_id(2)
    qt = qt_ref[0].astype(F32)
    row = lax.broadcasted_iota(jnp.int32, (LANES, tq), 0)
    q_aug = []
    for e in range(2):
        h = 2 * hp + e
        own = (row < HEAD_DIM) if e == 0 else (row >= HEAD_DIM)
        ones_rows = (row == h) | (row == n_heads + h) | (row == 2 * n_heads + h)
        q_aug.append(jnp.concatenate([jnp.where(own, qt, 0.0), jnp.where(ones_rows, 1.0, 0.0)], axis=0).astype(BF16))

    m_sc[...] = jnp.full(m_sc.shape, NEG, F32)
    l_sc[...] = jnp.zeros(l_sc.shape, F32)
    acc_sc[...] = jnp.zeros(acc_sc.shape, F32)

    def produce(j, slot):
        k0 = pl.multiple_of(j * tq, tq)
        ka = jnp.concatenate([k_ref[pl.ds(k0, tq), :], aug_ref[pl.ds(k0, tq), :]], axis=1)
        for e in range(2):
            s_sc[slot, e] = jnp.dot(ka, q_aug[e], preferred_element_type=F32)

    ones_rows = jnp.ones((BF16_ROWS, tq), BF16)

    def consume(j, slot, refs, masked):
        k0 = pl.multiple_of(j * tq, tq)
        vt = vt_ref[0, :, pl.ds(k0, tq)]
        new_refs = []
        for e in range(2):
            s = s_sc[slot, e]
            if masked:
                kpos = lax.broadcasted_iota(jnp.int32, (tq, tq), 0)
                qpos = lax.broadcasted_iota(jnp.int32, (tq, tq), 1)
                s = jnp.where(kpos <= qpos, s, NEG)
            ref = jnp.tile(refs[e], (1, tq // LANES))
            m_old = m_sc[e]
            m_new = jnp.maximum(m_old, jnp.max(s, axis=0, keepdims=True) - ref)
            alpha = jnp.exp2(m_old - m_new)
            p = jnp.exp2(s - (m_new + ref)).astype(BF16)
            v_aug = jnp.concatenate([vt[e * HEAD_DIM:(e + 1) * HEAD_DIM, :], ones_rows], axis=0)
            pv = jnp.dot(v_aug, p, preferred_element_type=F32)
            l_sc[e] = alpha * l_sc[e] + pv[HEAD_DIM:HEAD_DIM + 1]
            acc_sc[e] = alpha * acc_sc[e] + pv[:HEAD_DIM]
            m_sc[e] = m_new
            new_refs.append(refs[e] + tot_ref[0, j, pl.ds(2 * hp + e, 1), :])
        return tuple(new_refs)

    produce(0, 0)

    def pair(jj, refs):
        j = 2 * jj
        produce(j + 1, 1)
        refs = consume(j, 0, refs, False)
        produce(j + 2, 0)
        return consume(j + 1, 1, refs, False)

    zero_ref = jnp.zeros((1, LANES), F32)
    refs = lax.fori_loop(0, i // 2, pair, (zero_ref, zero_ref))

    @pl.when(i % 2 == 0)
    def _():
        consume(i, 0, refs, True)

    @pl.when(i % 2 == 1)
    def _():
        produce(i, 1)
        consume(i, 1, consume(i - 1, 0, refs, False), True)

    out_t = jnp.concatenate([acc_sc[0] / l_sc[0], acc_sc[1] / l_sc[1]], axis=0)
    o_ref[...] = out_t.T


def _attn_prompt(qt_bf, k_bf, aug_bf, vt_bf, tot, *, tq):
    batch, d_att, seq_len = qt_bf.shape
    n_heads = d_att // HEAD_DIM
    n_pairs = d_att // LANES
    nq = seq_len // tq
    assert seq_len % tq == 0 and tq % LANES == 0 and tot.shape == (batch, nq, n_heads, LANES)
    return pl.pallas_call(
        functools.partial(_attn_prompt_kernel, tq=tq, n_heads=n_heads),
        grid=(batch, n_pairs, nq),
        in_specs=[pl.BlockSpec((1, LANES, tq), lambda b, hp, i: (b, hp, i)),
                  pl.BlockSpec((seq_len, LANES), lambda b, hp, i: (b, hp)),
                  pl.BlockSpec((seq_len, LANES), lambda b, hp, i: (b, 0)),
                  pl.BlockSpec((1, LANES, seq_len), lambda b, hp, i: (b, hp, 0)),
                  pl.BlockSpec((1, nq, n_heads, LANES), lambda b, hp, i: (b, 0, 0, 0))],
        out_specs=pl.BlockSpec((tq, LANES), lambda b, hp, i: (b * nq + i, hp)),
        out_shape=jax.ShapeDtypeStruct((batch * seq_len, d_att), F32),
        scratch_shapes=[pltpu.VMEM((2, 1, tq), F32), pltpu.VMEM((2, 1, tq), F32), pltpu.VMEM((2, HEAD_DIM, tq), F32),
                        pltpu.VMEM((2, 2, tq, tq), F32)],
        compiler_params=_cparams("parallel", "parallel", "parallel"),
        name="attn_prompt",
    )(qt_bf, k_bf, aug_bf, vt_bf, tot)


def _attn_sample_kernel(pt_ref, q_ref, kn_ref, vn_ref, lfn_ref, *refs, pages_per_step, n_new):
    del pt_ref
    pg = pages_per_step
    k_refs, v_refs, lf_refs = refs[:pg], refs[pg:2 * pg], refs[2 * pg:3 * pg]
    o_ref, qbd_sc, m_sc, l_sc, acc_sc, coff_sc = refs[3 * pg:]
    g = pl.program_id(1)
    n_rows, d_att = qbd_sc.shape
    n_heads = d_att // HEAD_DIM
    row_head = lax.broadcasted_iota(jnp.int32, (n_rows, d_att), 0) % n_heads
    lane_head = lax.broadcasted_iota(jnp.int32, (n_rows, d_att), 1) // HEAD_DIM
    own = row_head == lane_head

    @pl.when(g == 0)
    def _init():
        qf = q_ref[0].astype(F32)
        rows = jnp.concatenate([jnp.broadcast_to(qf[i:i + 1], (n_heads, d_att)) for i in range(n_new)], axis=0)
        qbd_sc[...] = jnp.where(own, rows, 0.0).astype(BF16)
        m_sc[...] = jnp.full(m_sc.shape, NEG, F32)
        l_sc[...] = jnp.zeros(l_sc.shape, F32)
        acc_sc[...] = jnp.zeros(acc_sc.shape, F32)
        coff_sc[...] = jnp.zeros(coff_sc.shape, F32)

    lf = jnp.concatenate([lf_refs[p][0, 0] for p in range(pg)], axis=0)
    tri = (lax.broadcasted_iota(jnp.int32, (CHUNK, CHUNK), 0)
           <= lax.broadcasted_iota(jnp.int32, (CHUNK, CHUNK), 1)).astype(BF16)
    c_in = sum(jnp.dot(t.astype(BF16), tri, preferred_element_type=F32) for t in _split3_bf16(lf))
    tot = jnp.broadcast_to(c_in[:, CHUNK - 1:CHUNK], c_in.shape)
    page = lax.broadcasted_iota(jnp.int32, c_in.shape, 0) // n_heads
    incl = tot
    s = 1
    while s < pg:
        incl = incl + jnp.where(page >= s, pltpu.roll(incl, s * n_heads, 0), 0.0)
        s *= 2
    coff = coff_sc[...]
    ck = c_in + (incl - tot) + jnp.concatenate([coff] * pg, axis=0)
    coff_sc[...] = coff + incl[(pg - 1) * n_heads:, :]

    qbd = qbd_sc[...]
    s_parts = []
    for p in range(pg):
        ck_p = jnp.concatenate([ck[p * n_heads:(p + 1) * n_heads]] * n_new, axis=0)
        s_parts.append(jnp.dot(qbd, k_refs[p][0, 0].astype(BF16), preferred_element_type=F32) - ck_p)
    s_all = jnp.concatenate(s_parts, axis=1)
    m_old = m_sc[...]
    m_new = jnp.maximum(m_old, jnp.max(s_all, axis=1, keepdims=True))
    alpha = jnp.exp(m_old - m_new)
    p_all = jnp.exp(s_all - m_new)
    l_sc[...] = alpha * l_sc[...] + jnp.sum(p_all, axis=1, keepdims=True)
    pv = jnp.zeros(acc_sc.shape, F32)
    for p in range(pg):
        pv = pv + lax.dot_general(p_all[:, p * CHUNK:(p + 1) * CHUNK].astype(BF16), v_refs[p][0, 0].astype(BF16),
                                  NT_DIMS, preferred_element_type=F32)
    acc_sc[...] = alpha * acc_sc[...] + pv
    m_sc[...] = m_new

    @pl.when(g == pl.num_programs(1) - 1)
    def _finish():
        qf = qbd_sc[...].astype(F32)
        row_query = lax.broadcasted_iota(jnp.int32, (n_rows, 1), 0) // n_heads
        lfn = jnp.concatenate([lfn_ref[0]] * n_new, axis=0)
        c = jnp.concatenate([coff_sc[...]] * n_new, axis=0)[:, 0:1]
        m, l, acc = m_sc[...], l_sc[...], acc_sc[...]
        for j in range(n_new):
            c = c + lfn[:, j:j + 1]
            s = jnp.sum(qf * kn_ref[0, j:j + 1, :], axis=1, keepdims=True) - c
            s = jnp.where(row_query >= j, s, NEG)
            m_new = jnp.maximum(m, s)
            alpha = jnp.exp(m - m_new)
            pj = jnp.exp(s - m_new)
            l = alpha * l + pj
            acc = alpha * acc + pj * vn_ref[0, j:j + 1, :]
            m = m_new
        out = jnp.where(own, acc / l, 0.0)
        for i in range(n_new):
            o_ref[0, i:i + 1, :] = jnp.sum(out[i * n_heads:(i + 1) * n_heads], axis=0, keepdims=True)


def _attn_sample(page_table, q_bf, k_new, v_new, lf_new, cache_kt, cache_vt, cache_lft, *, layer, pages_per_step):
    n_seq, n_new, d_att = q_bf.shape
    n_heads = d_att // HEAD_DIM
    n_pages = page_table.shape[1]
    pg = pages_per_step
    assert n_pages % pg == 0 and cache_kt.shape[2:] == (d_att, CHUNK) and cache_lft.shape[2:] == (n_heads, CHUNK)
    n_rows = n_new * n_heads

    def page_spec(height, p):
        return pl.BlockSpec((1, 1, height, CHUNK), lambda s, g, pt: (layer, pt[s, g * pg + p], 0, 0))

    seq_spec = lambda a: pl.BlockSpec((1,) + a.shape[1:], lambda s, g, pt: (s, 0, 0))
    grid_spec = pltpu.PrefetchScalarGridSpec(
        num_scalar_prefetch=1,
        grid=(n_seq, n_pages // pg),
        in_specs=[seq_spec(q_bf), seq_spec(k_new), seq_spec(v_new), seq_spec(lf_new)]
        + [page_spec(d_att, p) for p in range(pg)]
        + [page_spec(d_att, p) for p in range(pg)]
        + [page_spec(n_heads, p) for p in range(pg)],
        out_specs=pl.BlockSpec((1, n_new, d_att), lambda s, g, pt: (s, 0, 0)),
        scratch_shapes=[pltpu.VMEM((n_rows, d_att), BF16), pltpu.VMEM((n_rows, 1), F32), pltpu.VMEM((n_rows, 1), F32),
                        pltpu.VMEM((n_rows, d_att), F32), pltpu.VMEM((n_heads, CHUNK), F32)],
    )
    return pl.pallas_call(
        functools.partial(_attn_sample_kernel, pages_per_step=pg, n_new=n_new),
        grid_spec=grid_spec,
        out_shape=jax.ShapeDtypeStruct((n_seq, n_new, d_att), F32),
        compiler_params=_cparams("parallel", "arbitrary"),
        name="attn_sample",
    )(page_table, q_bf, k_new, v_new, lf_new, *([cache_kt] * pg), *([cache_vt] * pg), *([cache_lft] * pg))


def _s5_tail(hs_bf, u, cc_ref, d_ref, wg_ref, bg_ref):
    y = jnp.dot(hs_bf, cc_ref[...], preferred_element_type=F32) + d_ref[...] * u
    z = jax.nn.gelu(y)
    gate = jax.nn.sigmoid(jnp.dot(z.astype(BF16), wg_ref[...], preferred_element_type=F32) + bg_ref[...])
    return z * gate


def _s5_prompt_kernel(u_ref, h0_ref, ab_ref, bb_ref, cc_ref, d_ref, wg_ref, bg_ref, o_ref, hT_ref,
                      h_sc, bu_sc, hs_sc, *, ts, n_state):
    t = pl.program_id(1)

    @pl.when(t == 0)
    def _():
        h_sc[...] = h0_ref[0]

    u = u_ref[...]
    bu_sc[...] = jnp.dot(u.astype(BF16), bb_ref[...], preferred_element_type=F32)
    a_re = ab_ref[0:1, :]
    a_im = ab_ref[1:2, :]

    def body(tt, carry):
        h_re, h_im = carry
        b_re = bu_sc[pl.ds(tt, 1), :n_state]
        b_im = bu_sc[pl.ds(tt, 1), n_state:]
        n_re = a_re * h_re - a_im * h_im + b_re
        n_im = a_re * h_im + a_im * h_re + b_im
        hs_sc[pl.ds(tt, 1), :n_state] = n_re
        hs_sc[pl.ds(tt, 1), n_state:] = n_im
        return n_re, n_im

    h_re, h_im = lax.fori_loop(0, ts, body, (h_sc[:, :n_state], h_sc[:, n_state:]), unroll=8)
    h_sc[:, :n_state] = h_re
    h_sc[:, n_state:] = h_im
    o_ref[...] = _s5_tail(hs_sc[...].astype(BF16), u, cc_ref, d_ref, wg_ref, bg_ref)

    @pl.when(t == pl.num_programs(1) - 1)
    def _():
        hT_ref[0] = h_sc[...]


def _s5_prompt(u, h0, ab, bb, cc, d, wg, bg, *, batch, seq_len, ts):
    n, d_ssm = u.shape
    n_state = ab.shape[1]
    nt = seq_len // ts
    assert seq_len % ts == 0
    consts = [ab, bb, cc, d, wg, bg]
    return pl.pallas_call(
        functools.partial(_s5_prompt_kernel, ts=ts, n_state=n_state),
        grid=(batch, nt),
        in_specs=[pl.BlockSpec((ts, d_ssm), lambda b, t: (b * nt + t, 0)),
                  pl.BlockSpec((1, 1, 2 * n_state), lambda b, t: (b, 0, 0))]
        + [_const_spec(c.shape) for c in consts],
        out_specs=[pl.BlockSpec((ts, d_ssm), lambda b, t: (b * nt + t, 0)),
                   pl.BlockSpec((1, 1, 2 * n_state), lambda b, t: (b, 0, 0))],
        out_shape=[jax.ShapeDtypeStruct((n, d_ssm), F32), jax.ShapeDtypeStruct((batch, 1, 2 * n_state), F32)],
        scratch_shapes=[pltpu.VMEM((1, 2 * n_state), F32), pltpu.VMEM((ts, 2 * n_state), F32),
                        pltpu.VMEM((ts, 2 * n_state), F32)],
        compiler_params=_cparams("parallel", "arbitrary"),
        name="s5_prompt",
    )(u, h0, *consts)


def _s5_sample_kernel(u_ref, h0_ref, ab_ref, bb_ref, cc_ref, d_ref, wg_ref, bg_ref, o_ref, hT_ref, *, n_state):
    a_re = ab_ref[0:1, :]
    a_im = ab_ref[1:2, :]
    h_re = h0_ref[:, :n_state]
    h_im = h0_ref[:, n_state:]
    for t in range(u_ref.shape[0]):
        u = u_ref[t]
        bu = jnp.dot(u, bb_ref[...], precision=HIGHEST, preferred_element_type=F32)
        h_re, h_im = (a_re * h_re - a_im * h_im + bu[:, :n_state],
                      a_re * h_im + a_im * h_re + bu[:, n_state:])
        hs = jnp.concatenate([h_re, h_im], axis=1)
        o_ref[t] = _s5_tail(hs.astype(BF16), u, cc_ref, d_ref, wg_ref, bg_ref)
    hT_ref[:, :n_state] = h_re
    hT_ref[:, n_state:] = h_im


def _s5_sample(u, h0, ab, bb_f32, cc, d, wg, bg):
    n_new, n_seq, d_ssm = u.shape
    n_state = ab.shape[1]
    return pl.pallas_call(
        functools.partial(_s5_sample_kernel, n_state=n_state),
        out_shape=[jax.ShapeDtypeStruct((n_new, n_seq, d_ssm), F32), jax.ShapeDtypeStruct((n_seq, 2 * n_state), F32)],
        compiler_params=pltpu.CompilerParams(vmem_limit_bytes=VMEM_LIMIT),
        name="s5_sample",
    )(u, h0, ab, bb_f32, cc, d, wg, bg)


def _post_kernel(h_ref, att_ref, mlp_ref, ssm_ref, pe_ref, gmo_ref, wo_ref, gf_ref, wg_ref, wu_ref, wd_ref,
                 gp_ref, wpg_ref, wpp_ref, o_ref, *, d_att, d_mlp):
    gmo = gmo_ref[...]
    e1 = d_att + d_mlp
    a = (_rms(att_ref[...]) * gmo[:, :d_att]).astype(BF16)
    m = (_rms(mlp_ref[...]) * gmo[:, d_att:e1]).astype(BF16)
    s = (_rms(ssm_ref[...]) * gmo[:, e1:]).astype(BF16)
    h = h_ref[...] + (jnp.dot(a, wo_ref[:d_att, :], preferred_element_type=F32)
                      + jnp.dot(m, wo_ref[d_att:e1, :], preferred_element_type=F32)
                      + jnp.dot(s, wo_ref[e1:, :], preferred_element_type=F32))
    hn = (_rms(h) * gf_ref[...]).astype(BF16)
    ff = jax.nn.silu(jnp.dot(hn, wg_ref[...], preferred_element_type=F32)) * jnp.dot(
        hn, wu_ref[...], preferred_element_type=F32)
    h = h + jnp.dot(ff.astype(BF16), wd_ref[...], preferred_element_type=F32)
    gate = jax.nn.sigmoid(jnp.dot((_rms(h) * gp_ref[...]).astype(BF16), wpg_ref[...], preferred_element_type=F32))
    o_ref[...] = h + jnp.dot(pe_ref[...].astype(BF16), wpp_ref[...], preferred_element_type=F32) * gate


def _post(h, att, mlp, ssm, pe, gmo, wo, gf, wg, wu, wd, gp, wpg, wpp, *, tm):
    n, d_model = h.shape
    assert n % tm == 0
    row = lambda a: pl.BlockSpec((tm, a.shape[1]), lambda i: (i, 0))
    consts = [gmo, wo, gf, wg, wu, wd, gp, wpg, wpp]
    return pl.pallas_call(
        functools.partial(_post_kernel, d_att=att.shape[1], d_mlp=mlp.shape[1]),
        grid=(n // tm,),
        in_specs=[row(h), row(att), row(mlp), row(ssm), row(pe)] + [_const_spec(c.shape) for c in consts],
        out_specs=row(h),
        out_shape=jax.ShapeDtypeStruct((n, d_model), F32),
        compiler_params=_cparams("parallel"),
        name="post",
    )(h, att, mlp, ssm, pe, *consts)


def _block_diag(blocks):
    g, r, c = blocks.shape
    eye = jnp.eye(g, dtype=blocks.dtype)
    return (eye[:, None, :, None] * blocks[:, :, None, :]).reshape(g * r, g * c)


def _s5_params(a_re, a_im, log_dt, b_re, b_im, c_re, c_im, d, w_glu, b_glu):
    dt = jnp.exp(log_dt)[:, None]
    mag = jnp.exp(a_re * dt)
    ab_re = mag * jnp.cos(a_im * dt)
    ab_im = mag * jnp.sin(a_im * dt)
    den = a_re * a_re + a_im * a_im
    coef_re = ((ab_re - 1.0) * a_re + ab_im * a_im) / den
    coef_im = (ab_im * a_re - (ab_re - 1.0) * a_im) / den
    bb_re = coef_re[..., None] * b_re - coef_im[..., None] * b_im
    bb_im = coef_re[..., None] * b_im + coef_im[..., None] * b_re
    ab = jnp.stack([ab_re.reshape(-1), ab_im.reshape(-1)])
    bb = jnp.concatenate([_block_diag(jnp.swapaxes(bb_re, 1, 2)), _block_diag(jnp.swapaxes(bb_im, 1, 2))], axis=1)
    cc = jnp.concatenate([_block_diag(jnp.swapaxes(c_re, 1, 2)), -_block_diag(jnp.swapaxes(c_im, 1, 2))], axis=0)
    return ab, bb, cc.astype(BF16), d[None, :], _block_diag(w_glu).astype(BF16), b_glu[None, :]


def kernel(x_prompt, x_sample, cache_k, cache_v, cache_logf, state_ssm_re, state_ssm_im, page_table, p_prompt, p_sample, g_norm_mix, w_in, b_forget, g_q, g_k, w_spatial, b_spatial, ssm_a_re, ssm_a_im, ssm_log_dt, ssm_b_re, ssm_b_im, ssm_c_re, ssm_c_im, ssm_d, w_glu, b_glu, g_mix_out, w_out, g_norm_ffn, w_ffn_gate, w_ffn_up, w_ffn_down, g_norm_ple, w_ple_gate, w_ple_proj):
    batch, seq_len, d_model = x_prompt.shape
    n_seq, n_new, _ = x_sample.shape
    depth, n_pool, page, n_heads, head_dim = cache_k.shape
    assert head_dim == HEAD_DIM and page == CHUNK and n_seq * n_new == CHUNK
    d_att = n_heads * HEAD_DIM
    n_mlp_heads = w_spatial.shape[1]
    d_mlp = n_mlp_heads * HEAD_DIM
    n_groups, n_st = ssm_a_re.shape[1:]
    d_ssm = n_groups * SSM_CH
    n_state = n_groups * n_st
    n_p = batch * seq_len
    n_s = n_seq * n_new

    tq = min(512, seq_len)
    ts = min(512, seq_len)
    tm_post = min(256, seq_len)
    pages_per_step = min(16, page_table.shape[1])

    cache_kt = jnp.transpose(cache_k, (0, 1, 3, 4, 2)).reshape(depth, n_pool, d_att, page)
    cache_vt = jnp.transpose(cache_v, (0, 1, 3, 4, 2)).reshape(depth, n_pool, d_att, page)
    cache_lft = jnp.swapaxes(cache_logf, 2, 3)

    gavg = _block_diag(jnp.full((n_heads, HEAD_DIM, HEAD_DIM), 1.0 / HEAD_DIM, F32)).astype(BF16)
    tril_new = jnp.tril(jnp.ones((n_new, n_new), F32))
    eye_seq = jnp.eye(n_seq, dtype=F32)

    hp = x_prompt.reshape(n_p, d_model)
    hs = x_sample.reshape(n_s, d_model)
    h0_prompt = jnp.zeros((batch, 1, 2 * n_state), F32)
    outs = {name: [] for name in ("kp", "vp", "lfp", "srp", "sip", "ks", "vs", "lfs", "srs", "sis", "mvs")}

    for i in range(depth):
        f0, f1 = 3 * d_att, 3 * d_att + n_heads
        w_q, w_k, w_v, w_f, w_rest = (w_in[i][:, :d_att], w_in[i][:, d_att:2 * d_att], w_in[i][:, 2 * d_att:f0],
                                      w_in[i][:, f0:f1], w_in[i][:, f1:])
        w_f_pad = jnp.concatenate([w_f, jnp.zeros((d_model, LANES - n_heads), F32)], axis=1)
        w1_p = jnp.concatenate([w_k, w_rest, w_f_pad], axis=1).astype(BF16)
        pad_rows = -(3 * d_att + n_heads) % BF16_ROWS
        w2t_p = jnp.concatenate([w_q.T, w_k.T, w_v.T, w_f.T, jnp.zeros((pad_rows, d_model), F32)], axis=0).astype(BF16)
        w1_s = jnp.concatenate([w_q, w_rest, w_f_pad, w_k, w_v], axis=1).astype(BF16)
        bf_row = jnp.concatenate([b_forget[i], jnp.zeros((LANES - n_heads,), F32)])[None, :]
        gq = jnp.tile(g_q[i], n_heads)
        gk = jnp.tile(g_k[i], n_heads)
        bs_p = jnp.repeat(b_spatial[i].T, HEAD_DIM, axis=1)
        ws_s = jnp.einsum("st,hij->hsitj", eye_seq, w_spatial[i][:, :n_new, :n_new] * tril_new).reshape(
            n_mlp_heads, n_s, n_s)
        bs_s = jnp.tile(jnp.repeat(b_spatial[i][:, :n_new].T, HEAD_DIM, axis=1), (n_seq, 1))
        ab, bb, cc, d_row, wglu_bd, bglu = _s5_params(
            ssm_a_re[i], ssm_a_im[i], ssm_log_dt[i], ssm_b_re[i], ssm_b_im[i], ssm_c_re[i], ssm_c_im[i],
            ssm_d[i], w_glu[i], b_glu[i])
        post_w = (g_mix_out[i][None, :], w_out[i].astype(BF16), g_norm_ffn[i][None, :], w_ffn_gate[i].astype(BF16),
                  w_ffn_up[i].astype(BF16), w_ffn_down[i].astype(BF16), g_norm_ple[i][None, :],
                  w_ple_gate[i].astype(BF16), w_ple_proj[i].astype(BF16))
        dims = dict(d_att=d_att, d_mlp=d_mlp, d_ssm=d_ssm)
        g_mix = g_norm_mix[i][None, :]

        qt_bf, k_bf, aug_bf, tot, kt, vt, vt_bf, lf, mlp, su = _proj_in(
            hp, [g_mix, w1_p, w2t_p, bf_row, b_forget[i][:, None], gq[:, None], gk[None, :], gk[:, None], gavg,
                 w_spatial[i], bs_p], tm=tq, seq_len=seq_len, **dims)
        att = _attn_prompt(qt_bf, k_bf, aug_bf, vt_bf, tot, tq=tq)
        ssm, h_t = _s5_prompt(su, h0_prompt, ab, bb.astype(BF16), cc, d_row, wglu_bd, bglu,
                              batch=batch, seq_len=seq_len, ts=ts)
        hp = _post(hp, att, mlp, ssm, p_prompt[i].reshape(n_p, -1), *post_w, tm=tm_post)
        outs["kp"].append(kt)
        outs["vp"].append(vt)
        outs["lfp"].append(lf)
        outs["srp"].append(h_t[:, 0, :n_state].reshape(batch, n_groups, n_st))
        outs["sip"].append(h_t[:, 0, n_state:].reshape(batch, n_groups, n_st))

        q_bf, k, v, lf_t, mlp, su, mv = _proj_in(
            hs, [g_mix, w1_s, bf_row, gq[None, :], gk[None, :], gavg, ws_s, bs_s], tm=n_s, **dims)
        lf_new = jnp.swapaxes(lf_t.reshape(n_heads, n_seq, n_new), 0, 1)
        att = _attn_sample(page_table, q_bf.reshape(n_seq, n_new, d_att), k.reshape(n_seq, n_new, d_att),
                           v.reshape(n_seq, n_new, d_att), lf_new, cache_kt, cache_vt, cache_lft,
                           layer=i, pages_per_step=pages_per_step)
        h0_s = jnp.concatenate([state_ssm_re[i].reshape(n_seq, n_state), state_ssm_im[i].reshape(n_seq, n_state)], axis=1)
        ssm, h_t = _s5_sample(jnp.swapaxes(su.reshape(n_seq, n_new, d_ssm), 0, 1), h0_s, ab, bb, cc, d_row, wglu_bd, bglu)
        ssm = jnp.swapaxes(ssm, 0, 1).reshape(n_s, d_ssm)
        hs = _post(hs, att.reshape(n_s, d_att), mlp, ssm, p_sample[i].reshape(n_s, -1), *post_w, tm=n_s)
        outs["ks"].append(k.reshape(n_seq, n_new, n_heads, HEAD_DIM))
        outs["vs"].append(v.reshape(n_seq, n_new, n_heads, HEAD_DIM))
        outs["lfs"].append(lf_t.T.reshape(n_seq, n_new, n_heads))
        outs["srs"].append(h_t[:, :n_state].reshape(n_seq, n_groups, n_st))
        outs["sis"].append(h_t[:, n_state:].reshape(n_seq, n_groups, n_st))
        outs["mvs"].append(mv.reshape(n_seq, n_new, d_mlp))

    st = {name: jnp.stack(v) for name, v in outs.items()}
    to_heads = lambda a: jnp.transpose(a.reshape(depth, batch, n_heads, HEAD_DIM, seq_len), (0, 1, 4, 2, 3))
    return (hp.reshape(batch, seq_len, d_model), hs.reshape(n_seq, n_new, d_model),
            to_heads(st["kp"]), to_heads(st["vp"]), jnp.swapaxes(st["lfp"], 2, 3), st["srp"], st["sip"],
            st["ks"], st["vs"], st["lfs"], st["srs"], st["sis"], st["mvs"])
```

```python
import functools

import jax
import jax.numpy as jnp
from jax import lax
from jax.experimental import pallas as pl
from jax.experimental.pallas import tpu as pltpu

F32 = jnp.float32
BF16 = jnp.bfloat16

HEAD_DIM = 64
CHUNK = 128
SSM_CH = 16
EPS = 1e-6
LANES = 128
BF16_ROWS = 16
NEG = -1e30
LOG2E = 1.4426950408889634
KV_TILES_PER_TRIP = 4

VMEM_LIMIT = 52 * 1024 * 1024
HIGHEST = lax.Precision.HIGHEST
NT_DIMS = (((1,), (1,)), ((), ()))


def _cparams(*sem):
    return pltpu.CompilerParams(dimension_semantics=sem, vmem_limit_bytes=VMEM_LIMIT)


def _rms(x):
    return x * lax.rsqrt(jnp.mean(x * x, axis=-1, keepdims=True) + EPS)


def _log_sigmoid(x):
    return jnp.minimum(x, 0.0) - jnp.log1p(jnp.exp(-jnp.abs(x)))


def _const_spec(shape):
    return pl.BlockSpec(shape, lambda *_: (0,) * len(shape), pipeline_mode=pl.Buffered(1))


def _split3_bf16(x):
    hi = x.astype(BF16).astype(F32)
    r = x - hi
    mid = r.astype(BF16).astype(F32)
    return hi, mid, (r - mid).astype(BF16).astype(F32)


def _proj_in_kernel(*refs, d_att, d_mlp, d_ssm, tm, prompt):
    if prompt:
        (h_ref, g_ref, w1_ref, w2t_ref, bfr_ref, bfc_ref, gqc_ref, gkr_ref, gkc_ref, gavg_ref, ws_ref, bs_ref,
         qt_ref, kb_ref, aug_ref, tot_ref, kt_ref, vt_ref, vtb_ref, lf_ref, mlp_ref, su_ref) = refs
    else:
        (h_ref, g_ref, w1_ref, bfr_ref, gqr_ref, gkr_ref, gavg_ref, ws_ref, bs_ref,
         q_ref, k_ref, v_ref, lf_ref, mlp_ref, su_ref, mv_ref) = refs
    n_heads = d_att // HEAD_DIM
    xn = (_rms(h_ref[...]) * g_ref[...]).astype(BF16)
    z = jnp.dot(xn, w1_ref[...], preferred_element_type=F32)
    mu = jax.nn.gelu(z[:, d_att:d_att + d_mlp])
    mv = jax.nn.gelu(z[:, d_att + d_mlp:d_att + 2 * d_mlp])
    o = d_att + 2 * d_mlp + d_ssm
    su_ref[...] = z[:, d_att + 2 * d_mlp:o]
    lane = lax.broadcasted_iota(jnp.int32, (tm, LANES), 1)
    lf_tok = jnp.where(lane < n_heads, _log_sigmoid(z[:, o:o + LANES] + bfr_ref[...]), 0.0)

    gavg = gavg_ref[...]

    def head_norm(t, g):
        ms = jnp.dot((t * t).astype(BF16), gavg, preferred_element_type=F32)
        return t * lax.rsqrt(ms + EPS) * g

    def head_norm_t(t, g):
        t3 = t.reshape(n_heads, HEAD_DIM, tm)
        ms = jnp.mean(t3 * t3, axis=1, keepdims=True)
        return (t3 * lax.rsqrt(ms + EPS)).reshape(d_att, tm) * g

    if prompt:
        kb_ref[...] = head_norm(z[:, :d_att], gkr_ref[...]).astype(BF16)
        row = lax.broadcasted_iota(jnp.int32, (tm, LANES), 0)
        c_loc = lf_tok
        s = 1
        while s < tm:
            c_loc = c_loc + jnp.where(row >= s, pltpu.roll(c_loc, s, 0), 0.0)
            s *= 2
        hi, mid, lo = _split3_bf16(-LOG2E * c_loc)
        aug_ref[...] = (hi + pltpu.roll(mid, n_heads, 1) + pltpu.roll(lo, 2 * n_heads, 1)).astype(BF16)

        f = lax.dot_general(w2t_ref[...], xn, NT_DIMS, preferred_element_type=F32)
        qt_ref[0] = (head_norm_t(f[:d_att], gqc_ref[...]) * (LOG2E * HEAD_DIM ** -0.5)).astype(BF16)
        kt_ref[0] = head_norm_t(f[d_att:2 * d_att], gkc_ref[...])
        vt = f[2 * d_att:3 * d_att]
        vt_ref[0] = vt
        vtb_ref[0] = vt.astype(BF16)
        lf_t = _log_sigmoid(f[3 * d_att:3 * d_att + n_heads] + bfc_ref[...])
        lf_ref[0] = lf_t
        tot_ref[0, 0] = jnp.broadcast_to(LOG2E * jnp.sum(lf_t, axis=1, keepdims=True), (n_heads, LANES))
    else:
        q_ref[...] = (head_norm(z[:, :d_att], gqr_ref[...]) * (HEAD_DIM ** -0.5)).astype(BF16)
        k_ref[...] = head_norm(z[:, o + LANES:o + LANES + d_att], gkr_ref[...])
        v_ref[...] = z[:, o + LANES + d_att:]
        lf_ref[...] = lf_tok.T[:n_heads, :]
        mv_ref[...] = mv

    n_mlp_heads = d_mlp // HEAD_DIM
    rows = lax.broadcasted_iota(jnp.int32, (CHUNK, CHUNK), 0)
    cols = lax.broadcasted_iota(jnp.int32, (CHUNK, CHUNK), 1)
    wst = [jnp.where(rows >= cols, ws_ref[h], 0.0).astype(BF16) for h in range(n_mlp_heads)]
    lane_head = lax.broadcasted_iota(jnp.int32, (CHUNK, d_mlp), 1) // HEAD_DIM
    for c in range(tm // CHUNK):
        sl = slice(c * CHUNK, (c + 1) * CHUNK)
        mvc = mv[sl].astype(BF16)
        mix = bs_ref[...]
        for h in range(n_mlp_heads):
            r = jnp.dot(wst[h], mvc, preferred_element_type=F32)
            mix = mix + jnp.where(lane_head == h, r, 0.0)
        mlp_ref[sl, :] = mu[sl] * mix


def _proj_in(h, consts, *, tm, d_att, d_mlp, d_ssm, seq_len=None):
    n, d_model = h.shape
    n_heads = d_att // HEAD_DIM
    prompt = seq_len is not None
    assert n % tm == 0 and tm % CHUNK == 0
    row = lambda width: pl.BlockSpec((tm, width), lambda i: (i, 0))
    sds = jax.ShapeDtypeStruct
    if prompt:
        batch, nt = n // seq_len, seq_len // tm
        assert seq_len % tm == 0
        col = lambda height: pl.BlockSpec((1, height, tm), lambda i: (i // nt, 0, i % nt))
        fm = lambda dt: sds((batch, d_att, seq_len), dt)
        out_shape = [fm(BF16), sds((n, d_att), BF16), sds((n, LANES), BF16), sds((batch, nt, n_heads, LANES), F32),
                     fm(F32), fm(F32), fm(BF16), sds((batch, n_heads, seq_len), F32),
                     sds((n, d_mlp), F32), sds((n, d_ssm), F32)]
        out_specs = [col(d_att), row(d_att), row(LANES),
                     pl.BlockSpec((1, 1, n_heads, LANES), lambda i: (i // nt, i % nt, 0, 0)),
                     col(d_att), col(d_att), col(d_att), col(n_heads), row(d_mlp), row(d_ssm)]
    else:
        out_shape = [sds((n, d_att), BF16), sds((n, d_att), F32), sds((n, d_att), F32), sds((n_heads, n), F32),
                     sds((n, d_mlp), F32), sds((n, d_ssm), F32), sds((n, d_mlp), F32)]
        out_specs = [row(d_att), row(d_att), row(d_att), pl.BlockSpec((n_heads, tm), lambda i: (0, i)),
                     row(d_mlp), row(d_ssm), row(d_mlp)]
    return pl.pallas_call(
        functools.partial(_proj_in_kernel, d_att=d_att, d_mlp=d_mlp, d_ssm=d_ssm, tm=tm, prompt=prompt),
        grid=(n // tm,),
        in_specs=[row(d_model)] + [_const_spec(c.shape) for c in consts],
        out_specs=out_specs,
        out_shape=out_shape,
        compiler_params=_cparams("parallel"),
        name="proj_in",
    )(h, *consts)


def _attn_prompt_kernel(qt_ref, k_ref, aug_ref, vt_ref, tot_ref, o_ref, m_sc, l_sc, acc_sc, s_sc, *, tq, n_heads):
    hp = pl.program_id(1)
    i = pl.program_id(2)
    qt = qt_ref[0].astype(F32)
    row = lax.broadcasted_iota(jnp.int32, (LANES, tq), 0)
    q_aug = []
    for e in range(2):
        h = 2 * hp + e
        own = (row < HEAD_DIM) if e == 0 else (row >= HEAD_DIM)
        ones_rows = (row == h) | (row == n_heads + h) | (row == 2 * n_heads + h)
        q_aug.append(jnp.concatenate([jnp.where(own, qt, 0.0), jnp.where(ones_rows, 1.0, 0.0)], axis=0).astype(BF16))

    m_sc[...] = jnp.full(m_sc.shape, NEG, F32)
    l_sc[...] = jnp.zeros(l_sc.shape, F32)
    acc_sc[...] = jnp.zeros(acc_sc.shape, F32)

    def produce(j, slot):
        k0 = pl.multiple_of(j * tq, tq)
        ka = jnp.concatenate([k_ref[pl.ds(k0, tq), :], aug_ref[pl.ds(k0, tq), :]], axis=1)
        for e in range(2):
            s_sc[slot, e] = jnp.dot(ka, q_aug[e], preferred_element_type=F32)

    ones_rows = jnp.ones((BF16_ROWS, tq), BF16)

    def consume(j, slot, refs, masked):
        k0 = pl.multiple_of(j * tq, tq)
        vt = vt_ref[0, :, pl.ds(k0, tq)]
        new_refs = []
        for e in range(2):
            s = s_sc[slot, e]
            if masked:
                kpos = lax.broadcasted_iota(jnp.int32, (tq, tq), 0)
                qpos = lax.broadcasted_iota(jnp.int32, (tq, tq), 1)
                s = jnp.where(kpos <= qpos, s, NEG)
            ref = jnp.tile(refs[e], (1, tq // LANES))
            m_old = m_sc[e]
            m_new = jnp.maximum(m_old, jnp.max(s, axis=0, keepdims=True) - ref)
            alpha = jnp.exp2(m_old - m_new)
            p = jnp.exp2(s - (m_new + ref)).astype(BF16)
            v_aug = jnp.concatenate([vt[e * HEAD_DIM:(e + 1) * HEAD_DIM, :], ones_rows], axis=0)
            pv = jnp.dot(v_aug, p, preferred_element_type=F32)
            l_sc[e] = alpha * l_sc[e] + pv[HEAD_DIM:HEAD_DIM + 1]
            acc_sc[e] = alpha * acc_sc[e] + pv[:HEAD_DIM]
            m_sc[e] = m_new
            new_refs.append(refs[e] + tot_ref[0, j, pl.ds(2 * hp + e, 1), :])
        return tuple(new_refs)

    produce(0, 0)

    def trip(jj, refs):
        j = KV_TILES_PER_TRIP * jj
        for k in range(KV_TILES_PER_TRIP):
            produce(j + k + 1, (k + 1) % 2)
            refs = consume(j + k, k % 2, refs, False)
        return refs

    zero_ref = jnp.zeros((1, LANES), F32)
    n_trips = i // KV_TILES_PER_TRIP
    refs = lax.fori_loop(0, n_trips, trip, (zero_ref, zero_ref))
    base = n_trips * KV_TILES_PER_TRIP

    for rest in range(KV_TILES_PER_TRIP):
        @pl.when(i - base == rest)
        def _(rest=rest):
            r = refs
            for k in range(rest):
                produce(base + k + 1, (k + 1) % 2)
                r = consume(base + k, k % 2, r, False)
            consume(i, rest % 2, r, True)

    out_t = jnp.concatenate([acc_sc[0] / l_sc[0], acc_sc[1] / l_sc[1]], axis=0)
    o_ref[...] = out_t.T


def _attn_prompt(qt_bf, k_bf, aug_bf, vt_bf, tot, *, tq):
    batch, d_att, seq_len = qt_bf.shape
    n_heads = d_att // HEAD_DIM
    n_pairs = d_att // LANES
    nq = seq_len // tq
    assert seq_len % tq == 0 and tq % LANES == 0 and tot.shape == (batch, nq, n_heads, LANES)
    return pl.pallas_call(
        functools.partial(_attn_prompt_kernel, tq=tq, n_heads=n_heads),
        grid=(batch, n_pairs, nq),
        in_specs=[pl.BlockSpec((1, LANES, tq), lambda b, hp, i: (b, hp, i)),
                  pl.BlockSpec((seq_len, LANES), lambda b, hp, i: (b, hp)),
                  pl.BlockSpec((seq_len, LANES), lambda b, hp, i: (b, 0)),
                  pl.BlockSpec((1, LANES, seq_len), lambda b, hp, i: (b, hp, 0)),
                  pl.BlockSpec((1, nq, n_heads, LANES), lambda b, hp, i: (b, 0, 0, 0))],
        out_specs=pl.BlockSpec((tq, LANES), lambda b, hp, i: (b * nq + i, hp)),
        out_shape=jax.ShapeDtypeStruct((batch * seq_len, d_att), F32),
        scratch_shapes=[pltpu.VMEM((2, 1, tq), F32), pltpu.VMEM((2, 1, tq), F32), pltpu.VMEM((2, HEAD_DIM, tq), F32),
                        pltpu.VMEM((2, 2, tq, tq), F32)],
        compiler_params=_cparams("parallel", "parallel", "parallel"),
        name="attn_prompt",
    )(qt_bf, k_bf, aug_bf, vt_bf, tot)


def _attn_sample_kernel(pt_ref, q_ref, kn_ref, vn_ref, lfn_ref, *refs, pages_per_step, n_new):
    del pt_ref
    pg = pages_per_step
    k_refs, v_refs, lf_refs = refs[:pg], refs[pg:2 * pg], refs[2 * pg:3 * pg]
    o_ref, qbd_sc, m_sc, l_sc, acc_sc, coff_sc = refs[3 * pg:]
    g = pl.program_id(1)
    n_rows, d_att = qbd_sc.shape
    n_heads = d_att // HEAD_DIM
    row_head = lax.broadcasted_iota(jnp.int32, (n_rows, d_att), 0) % n_heads
    lane_head = lax.broadcasted_iota(jnp.int32, (n_rows, d_att), 1) // HEAD_DIM
    own = row_head == lane_head

    @pl.when(g == 0)
    def _init():
        qf = q_ref[0].astype(F32)
        rows = jnp.concatenate([jnp.broadcast_to(qf[i:i + 1], (n_heads, d_att)) for i in range(n_new)], axis=0)
        qbd_sc[...] = jnp.where(own, rows, 0.0).astype(BF16)
        m_sc[...] = jnp.full(m_sc.shape, NEG, F32)
        l_sc[...] = jnp.zeros(l_sc.shape, F32)
        acc_sc[...] = jnp.zeros(acc_sc.shape, F32)
        coff_sc[...] = jnp.zeros(coff_sc.shape, F32)

    lf = jnp.concatenate([lf_refs[p][0, 0] for p in range(pg)], axis=0)
    tri = (lax.broadcasted_iota(jnp.int32, (CHUNK, CHUNK), 0)
           <= lax.broadcasted_iota(jnp.int32, (CHUNK, CHUNK), 1)).astype(BF16)
    c_in = sum(jnp.dot(t.astype(BF16), tri, preferred_element_type=F32) for t in _split3_bf16(lf))
    tot = jnp.broadcast_to(c_in[:, CHUNK - 1:CHUNK], c_in.shape)
    page = lax.broadcasted_iota(jnp.int32, c_in.shape, 0) // n_heads
    incl = tot
    s = 1
    while s < pg:
        incl = incl + jnp.where(page >= s, pltpu.roll(incl, s * n_heads, 0), 0.0)
        s *= 2
    coff = coff_sc[...]
    ck = c_in + (incl - tot) + jnp.concatenate([coff] * pg, axis=0)
    coff_sc[...] = coff + incl[(pg - 1) * n_heads:, :]

    qbd = qbd_sc[...]
    s_parts = []
    for p in range(pg):
        ck_p = jnp.concatenate([ck[p * n_heads:(p + 1) * n_heads]] * n_new, axis=0)
        s_parts.append(jnp.dot(qbd, k_refs[p][0, 0].astype(BF16), preferred_element_type=F32) - ck_p)
    s_all = jnp.concatenate(s_parts, axis=1)
    m_old = m_sc[...]
    m_new = jnp.maximum(m_old, jnp.max(s_all, axis=1, keepdims=True))
    alpha = jnp.exp(m_old - m_new)
    p_all = jnp.exp(s_all - m_new)
    l_sc[...] = alpha * l_sc[...] + jnp.sum(p_all, axis=1, keepdims=True)
    pv = jnp.zeros(acc_sc.shape, F32)
    for p in range(pg):
        pv = pv + lax.dot_general(p_all[:, p * CHUNK:(p + 1) * CHUNK].astype(BF16), v_refs[p][0, 0].astype(BF16),
                                  NT_DIMS, preferred_element_type=F32)
    acc_sc[...] = alpha * acc_sc[...] + pv
    m_sc[...] = m_new

    @pl.when(g == pl.num_programs(1) - 1)
    def _finish():
        qf = qbd_sc[...].astype(F32)
        row_query = lax.broadcasted_iota(jnp.int32, (n_rows, 1), 0) // n_heads
        lfn = jnp.concatenate([lfn_ref[0]] * n_new, axis=0)
        c = jnp.concatenate([coff_sc[...]] * n_new, axis=0)[:, 0:1]
        m, l, acc = m_sc[...], l_sc[...], acc_sc[...]
        for j in range(n_new):
            c = c + lfn[:, j:j + 1]
            s = jnp.sum(qf * kn_ref[0, j:j + 1, :], axis=1, keepdims=True) - c
            s = jnp.where(row_query >= j, s, NEG)
            m_new = jnp.maximum(m, s)
            alpha = jnp.exp(m - m_new)
            pj = jnp.exp(s - m_new)
            l = alpha * l + pj
            acc = alpha * acc + pj * vn_ref[0, j:j + 1, :]
            m = m_new
        out = jnp.where(own, acc / l, 0.0)
        for i in range(n_new):
            o_ref[0, i:i + 1, :] = jnp.sum(out[i * n_heads:(i + 1) * n_heads], axis=0, keepdims=True)


def _attn_sample(page_table, q_bf, k_new, v_new, lf_new, cache_kt, cache_vt, cache_lft, *, layer, pages_per_step):
    n_seq, n_new, d_att = q_bf.shape
    n_heads = d_att // HEAD_DIM
    n_pages = page_table.shape[1]
    pg = pages_per_step
    assert n_pages % pg == 0 and cache_kt.shape[2:] == (d_att, CHUNK) and cache_lft.shape[2:] == (n_heads, CHUNK)
    n_rows = n_new * n_heads

    def page_spec(height, p):
        return pl.BlockSpec((1, 1, height, CHUNK), lambda s, g, pt: (layer, pt[s, g * pg + p], 0, 0))

    seq_spec = lambda a: pl.BlockSpec((1,) + a.shape[1:], lambda s, g, pt: (s, 0, 0))
    grid_spec = pltpu.PrefetchScalarGridSpec(
        num_scalar_prefetch=1,
        grid=(n_seq, n_pages // pg),
        in_specs=[seq_spec(q_bf), seq_spec(k_new), seq_spec(v_new), seq_spec(lf_new)]
        + [page_spec(d_att, p) for p in range(pg)]
        + [page_spec(d_att, p) for p in range(pg)]
        + [page_spec(n_heads, p) for p in range(pg)],
        out_specs=pl.BlockSpec((1, n_new, d_att), lambda s, g, pt: (s, 0, 0)),
        scratch_shapes=[pltpu.VMEM((n_rows, d_att), BF16), pltpu.VMEM((n_rows, 1), F32), pltpu.VMEM((n_rows, 1), F32),
                        pltpu.VMEM((n_rows, d_att), F32), pltpu.VMEM((n_heads, CHUNK), F32)],
    )
    return pl.pallas_call(
        functools.partial(_attn_sample_kernel, pages_per_step=pg, n_new=n_new),
        grid_spec=grid_spec,
        out_shape=jax.ShapeDtypeStruct((n_seq, n_new, d_att), F32),
        compiler_params=_cparams("parallel", "arbitrary"),
        name="attn_sample",
    )(page_table, q_bf, k_new, v_new, lf_new, *([cache_kt] * pg), *([cache_vt] * pg), *([cache_lft] * pg))


def _s5_glu(y, wg_ref, bg_ref):
    z = jax.nn.gelu(y)
    gate = jax.nn.sigmoid(jnp.dot(z.astype(BF16), wg_ref[...], preferred_element_type=F32) + bg_ref[...])
    return z * gate


def _s5_prompt_kernel(u_ref, h0_ref, ab_ref, bb_ref, cc_ref, d_ref, wg_ref, bg_ref, o_ref, hT_ref,
                      h_sc, bu_sc, hs_sc, *, ts, n_state):
    t = pl.program_id(1)

    @pl.when(t == 0)
    def _():
        h_sc[...] = h0_ref[0]

    u = u_ref[...]
    bu_sc[...] = jnp.dot(u.astype(BF16), bb_ref[...], preferred_element_type=F32)
    a_re = ab_ref[0:1, :]
    a_im = ab_ref[1:2, :]

    def body(tt, carry):
        h_re, h_im = carry
        b_re = bu_sc[pl.ds(tt, 1), :n_state]
        b_im = bu_sc[pl.ds(tt, 1), n_state:]
        n_re = a_re * h_re - a_im * h_im + b_re
        n_im = a_re * h_im + a_im * h_re + b_im
        hs_sc[pl.ds(tt, 1), :n_state] = n_re
        hs_sc[pl.ds(tt, 1), n_state:] = n_im
        return n_re, n_im

    h_re, h_im = lax.fori_loop(0, ts, body, (h_sc[:, :n_state], h_sc[:, n_state:]), unroll=8)
    h_sc[:, :n_state] = h_re
    h_sc[:, n_state:] = h_im
    y = jnp.dot(hs_sc[...].astype(BF16), cc_ref[...], preferred_element_type=F32) + d_ref[...] * u
    o_ref[...] = _s5_glu(y, wg_ref, bg_ref)

    @pl.when(t == pl.num_programs(1) - 1)
    def _():
        hT_ref[0] = h_sc[...]


def _s5_prompt(u, h0, ab, bb, cc, d, wg, bg, *, batch, seq_len, ts):
    n, d_ssm = u.shape
    n_state = ab.shape[1]
    nt = seq_len // ts
    assert seq_len % ts == 0
    consts = [ab, bb, cc, d, wg, bg]
    return pl.pallas_call(
        functools.partial(_s5_prompt_kernel, ts=ts, n_state=n_state),
        grid=(batch, nt),
        in_specs=[pl.BlockSpec((ts, d_ssm), lambda b, t: (b * nt + t, 0)),
                  pl.BlockSpec((1, 1, 2 * n_state), lambda b, t: (b, 0, 0))]
        + [_const_spec(c.shape) for c in consts],
        out_specs=[pl.BlockSpec((ts, d_ssm), lambda b, t: (b * nt + t, 0)),
                   pl.BlockSpec((1, 1, 2 * n_state), lambda b, t: (b, 0, 0))],
        out_shape=[jax.ShapeDtypeStruct((n, d_ssm), F32), jax.ShapeDtypeStruct((batch, 1, 2 * n_state), F32)],
        scratch_shapes=[pltpu.VMEM((1, 2 * n_state), F32), pltpu.VMEM((ts, 2 * n_state), F32),
                        pltpu.VMEM((ts, 2 * n_state), F32)],
        compiler_params=_cparams("parallel", "arbitrary"),
        name="s5_prompt",
    )(u, h0, *consts)


def _s5_sample_kernel(u_ref, h0_ref, ab_ref, bb_ref, cc_ref, d_ref, wg_ref, bg_ref, o_ref, hT_ref, *, n_state):
    a_re = ab_ref[0:1, :]
    a_im = ab_ref[1:2, :]
    h_re = h0_ref[:, :n_state]
    h_im = h0_ref[:, n_state:]
    for t in range(u_ref.shape[0]):
        u = u_ref[t]
        bu = jnp.dot(u, bb_ref[...], precision=HIGHEST, preferred_element_type=F32)
        h_re, h_im = (a_re * h_re - a_im * h_im + bu[:, :n_state],
                      a_re * h_im + a_im * h_re + bu[:, n_state:])
        hs = jnp.concatenate([h_re, h_im], axis=1).astype(BF16)
        y = jnp.dot(hs, cc_ref[...], preferred_element_type=F32) + d_ref[...] * u
        o_ref[t] = _s5_glu(y, wg_ref, bg_ref)
    hT_ref[:, :n_state] = h_re
    hT_ref[:, n_state:] = h_im


def _s5_sample(u, h0, ab, bb_f32, cc, d, wg, bg):
    n_new, n_seq, d_ssm = u.shape
    n_state = ab.shape[1]
    return pl.pallas_call(
        functools.partial(_s5_sample_kernel, n_state=n_state),
        out_shape=[jax.ShapeDtypeStruct((n_new, n_seq, d_ssm), F32), jax.ShapeDtypeStruct((n_seq, 2 * n_state), F32)],
        compiler_params=pltpu.CompilerParams(vmem_limit_bytes=VMEM_LIMIT),
        name="s5_sample",
    )(u, h0, ab, bb_f32, cc, d, wg, bg)


def _post_kernel(h_ref, att_ref, mlp_ref, ssm_ref, pe_ref, gmo_ref, wo_ref, gf_ref, wg_ref, wu_ref, wd_ref,
                 gp_ref, wpg_ref, wpp_ref, o_ref, *, d_att, d_mlp):
    gmo = gmo_ref[...]
    e1 = d_att + d_mlp
    a = (_rms(att_ref[...]) * gmo[:, :d_att]).astype(BF16)
    m = (_rms(mlp_ref[...]) * gmo[:, d_att:e1]).astype(BF16)
    s = (_rms(ssm_ref[...]) * gmo[:, e1:]).astype(BF16)
    h = h_ref[...] + (jnp.dot(a, wo_ref[:d_att, :], preferred_element_type=F32)
                      + jnp.dot(m, wo_ref[d_att:e1, :], preferred_element_type=F32)
                      + jnp.dot(s, wo_ref[e1:, :], preferred_element_type=F32))
    hn = (_rms(h) * gf_ref[...]).astype(BF16)
    ff = jax.nn.silu(jnp.dot(hn, wg_ref[...], preferred_element_type=F32)) * jnp.dot(
        hn, wu_ref[...], preferred_element_type=F32)
    h = h + jnp.dot(ff.astype(BF16), wd_ref[...], preferred_element_type=F32)
    gate = jax.nn.sigmoid(jnp.dot((_rms(h) * gp_ref[...]).astype(BF16), wpg_ref[...], preferred_element_type=F32))
    o_ref[...] = h + jnp.dot(pe_ref[0].astype(BF16), wpp_ref[...], preferred_element_type=F32) * gate


def _post(h, att, mlp, ssm, pe, gmo, wo, gf, wg, wu, wd, gp, wpg, wpp, *, tm, pe_layer):
    n, d_model = h.shape
    assert n % tm == 0
    row = lambda a: pl.BlockSpec((tm, a.shape[1]), lambda i: (i, 0))
    consts = [gmo, wo, gf, wg, wu, wd, gp, wpg, wpp]
    return pl.pallas_call(
        functools.partial(_post_kernel, d_att=att.shape[1], d_mlp=mlp.shape[1]),
        grid=(n // tm,),
        in_specs=[row(h), row(att), row(mlp), row(ssm), pl.BlockSpec((1, tm, pe.shape[2]), lambda i: (pe_layer, i, 0))]
        + [_const_spec(c.shape) for c in consts],
        out_specs=row(h),
        out_shape=jax.ShapeDtypeStruct((n, d_model), F32),
        compiler_params=_cparams("parallel"),
        name="post",
    )(h, att, mlp, ssm, pe, *consts)


def _block_diag(blocks):
    g, r, c = blocks.shape
    eye = jnp.eye(g, dtype=blocks.dtype)
    return (eye[:, None, :, None] * blocks[:, :, None, :]).reshape(g * r, g * c)


def _s5_params(a_re, a_im, log_dt, b_re, b_im, c_re, c_im, d, w_glu, b_glu):
    dt = jnp.exp(log_dt)[:, None]
    mag = jnp.exp(a_re * dt)
    ab_re = mag * jnp.cos(a_im * dt)
    ab_im = mag * jnp.sin(a_im * dt)
    den = a_re * a_re + a_im * a_im
    coef_re = ((ab_re - 1.0) * a_re + ab_im * a_im) / den
    coef_im = (ab_im * a_re - (ab_re - 1.0) * a_im) / den
    bb_re = coef_re[..., None] * b_re - coef_im[..., None] * b_im
    bb_im = coef_re[..., None] * b_im + coef_im[..., None] * b_re
    ab = jnp.stack([ab_re.reshape(-1), ab_im.reshape(-1)])
    bb = jnp.concatenate([_block_diag(jnp.swapaxes(bb_re, 1, 2)), _block_diag(jnp.swapaxes(bb_im, 1, 2))], axis=1)
    cc = jnp.concatenate([_block_diag(jnp.swapaxes(c_re, 1, 2)), -_block_diag(jnp.swapaxes(c_im, 1, 2))], axis=0)
    return ab, bb, cc.astype(BF16), d[None, :], _block_diag(w_glu).astype(BF16), b_glu[None, :]


def kernel(x_prompt, x_sample, cache_k, cache_v, cache_logf, state_ssm_re, state_ssm_im, page_table, p_prompt, p_sample, g_norm_mix, w_in, b_forget, g_q, g_k, w_spatial, b_spatial, ssm_a_re, ssm_a_im, ssm_log_dt, ssm_b_re, ssm_b_im, ssm_c_re, ssm_c_im, ssm_d, w_glu, b_glu, g_mix_out, w_out, g_norm_ffn, w_ffn_gate, w_ffn_up, w_ffn_down, g_norm_ple, w_ple_gate, w_ple_proj):
    batch, seq_len, d_model = x_prompt.shape
    n_seq, n_new, _ = x_sample.shape
    depth, n_pool, page, n_heads, head_dim = cache_k.shape
    assert head_dim == HEAD_DIM and page == CHUNK and n_seq * n_new == CHUNK
    d_att = n_heads * HEAD_DIM
    n_mlp_heads = w_spatial.shape[1]
    d_mlp = n_mlp_heads * HEAD_DIM
    n_groups, n_st = ssm_a_re.shape[1:]
    d_ssm = n_groups * SSM_CH
    n_state = n_groups * n_st
    n_p = batch * seq_len
    n_s = n_seq * n_new

    tq = min(512, seq_len)
    ts = min(512, seq_len)
    tm_post = min(256, seq_len)
    pages_per_step = min(16, page_table.shape[1])

    cache_kt = jnp.transpose(cache_k, (0, 1, 3, 4, 2)).reshape(depth, n_pool, d_att, page)
    cache_vt = jnp.transpose(cache_v, (0, 1, 3, 4, 2)).reshape(depth, n_pool, d_att, page)
    cache_lft = jnp.swapaxes(cache_logf, 2, 3)

    gavg = _block_diag(jnp.full((n_heads, HEAD_DIM, HEAD_DIM), 1.0 / HEAD_DIM, F32)).astype(BF16)
    tril_new = jnp.tril(jnp.ones((n_new, n_new), F32))
    eye_seq = jnp.eye(n_seq, dtype=F32)

    hp = x_prompt.reshape(n_p, d_model)
    hs = x_sample.reshape(n_s, d_model)
    h0_prompt = jnp.zeros((batch, 1, 2 * n_state), F32)
    pe_prompt = p_prompt.reshape(depth, n_p, -1)
    pe_sample = p_sample.reshape(depth, n_s, -1)
    outs = {name: [] for name in ("kp", "vp", "lfp", "srp", "sip", "ks", "vs", "lfs", "srs", "sis", "mvs")}

    for i in range(depth):
        f0, f1 = 3 * d_att, 3 * d_att + n_heads
        w_q, w_k, w_v, w_f, w_rest = (w_in[i][:, :d_att], w_in[i][:, d_att:2 * d_att], w_in[i][:, 2 * d_att:f0],
                                      w_in[i][:, f0:f1], w_in[i][:, f1:])
        w_f_pad = jnp.concatenate([w_f, jnp.zeros((d_model, LANES - n_heads), F32)], axis=1)
        w1_p = jnp.concatenate([w_k, w_rest, w_f_pad], axis=1).astype(BF16)
        pad_rows = -(3 * d_att + n_heads) % BF16_ROWS
        w2t_p = jnp.concatenate([w_q.T, w_k.T, w_v.T, w_f.T, jnp.zeros((pad_rows, d_model), F32)], axis=0).astype(BF16)
        w1_s = jnp.concatenate([w_q, w_rest, w_f_pad, w_k, w_v], axis=1).astype(BF16)
        bf_row = jnp.concatenate([b_forget[i], jnp.zeros((LANES - n_heads,), F32)])[None, :]
        gq = jnp.tile(g_q[i], n_heads)
        gk = jnp.tile(g_k[i], n_heads)
        bs_p = jnp.repeat(b_spatial[i].T, HEAD_DIM, axis=1)
        ws_s = jnp.einsum("st,hij->hsitj", eye_seq, w_spatial[i][:, :n_new, :n_new] * tril_new).reshape(
            n_mlp_heads, n_s, n_s)
        bs_s = jnp.tile(jnp.repeat(b_spatial[i][:, :n_new].T, HEAD_DIM, axis=1), (n_seq, 1))
        ab, bb, cc, d_row, wglu_bd, bglu = _s5_params(
            ssm_a_re[i], ssm_a_im[i], ssm_log_dt[i], ssm_b_re[i], ssm_b_im[i], ssm_c_re[i], ssm_c_im[i],
            ssm_d[i], w_glu[i], b_glu[i])
        post_w = (g_mix_out[i][None, :], w_out[i].astype(BF16), g_norm_ffn[i][None, :], w_ffn_gate[i].astype(BF16),
                  w_ffn_up[i].astype(BF16), w_ffn_down[i].astype(BF16), g_norm_ple[i][None, :],
                  w_ple_gate[i].astype(BF16), w_ple_proj[i].astype(BF16))
        dims = dict(d_att=d_att, d_mlp=d_mlp, d_ssm=d_ssm)
        g_mix = g_norm_mix[i][None, :]

        qt_bf, k_bf, aug_bf, tot, kt, vt, vt_bf, lf, mlp, su = _proj_in(
            hp, [g_mix, w1_p, w2t_p, bf_row, b_forget[i][:, None], gq[:, None], gk[None, :], gk[:, None], gavg,
                 w_spatial[i], bs_p], tm=tq, seq_len=seq_len, **dims)
        att = _attn_prompt(qt_bf, k_bf, aug_bf, vt_bf, tot, tq=tq)
        ssm, h_t = _s5_prompt(su, h0_prompt, ab, bb.astype(BF16), cc, d_row, wglu_bd, bglu,
                              batch=batch, seq_len=seq_len, ts=ts)
        hp = _post(hp, att, mlp, ssm, pe_prompt, *post_w, tm=tm_post, pe_layer=i)
        outs["kp"].append(kt)
        outs["vp"].append(vt)
        outs["lfp"].append(lf)
        outs["srp"].append(h_t[:, 0, :n_state].reshape(batch, n_groups, n_st))
        outs["sip"].append(h_t[:, 0, n_state:].reshape(batch, n_groups, n_st))

        q_bf, k, v, lf_t, mlp, su, mv = _proj_in(
            hs, [g_mix, w1_s, bf_row, gq[None, :], gk[None, :], gavg, ws_s, bs_s], tm=n_s, **dims)
        lf_new = jnp.swapaxes(lf_t.reshape(n_heads, n_seq, n_new), 0, 1)
        att = _attn_sample(page_table, q_bf.reshape(n_seq, n_new, d_att), k.reshape(n_seq, n_new, d_att),
                           v.reshape(n_seq, n_new, d_att), lf_new, cache_kt, cache_vt, cache_lft,
                           layer=i, pages_per_step=pages_per_step)
        h0_s = jnp.concatenate([state_ssm_re[i].reshape(n_seq, n_state), state_ssm_im[i].reshape(n_seq, n_state)], axis=1)
        ssm, h_t = _s5_sample(jnp.swapaxes(su.reshape(n_seq, n_new, d_ssm), 0, 1), h0_s, ab, bb, cc, d_row, wglu_bd, bglu)
        ssm = jnp.swapaxes(ssm, 0, 1).reshape(n_s, d_ssm)
        hs = _post(hs, att.reshape(n_s, d_att), mlp, ssm, pe_sample, *post_w, tm=n_s, pe_layer=i)
        outs["ks"].append(k.reshape(n_seq, n_new, n_heads, HEAD_DIM))
        outs["vs"].append(v.reshape(n_seq, n_new, n_heads, HEAD_DIM))
        outs["lfs"].append(lf_t.T.reshape(n_seq, n_new, n_heads))
        outs["srs"].append(h_t[:, :n_state].reshape(n_seq, n_groups, n_st))
        outs["sis"].append(h_t[:, n_state:].reshape(n_seq, n_groups, n_st))
        outs["mvs"].append(mv.reshape(n_seq, n_new, d_mlp))

    st = {name: jnp.stack(v) for name, v in outs.items()}
    to_heads = lambda a: jnp.transpose(a.reshape(depth, batch, n_heads, HEAD_DIM, seq_len), (0, 1, 4, 2, 3))
    return (hp.reshape(batch, seq_len, d_model), hs.reshape(n_seq, n_new, d_model),
            to_heads(st["kp"]), to_heads(st["vp"]), jnp.swapaxes(st["lfp"], 2, 3), st["srp"], st["sip"],
            st["ks"], st["vs"], st["lfs"], st["srs"], st["sis"], st["mvs"])
```

```python
import functools

import jax
import jax.numpy as jnp
from jax import lax
from jax.experimental import pallas as pl
from jax.experimental.pallas import tpu as pltpu

F32 = jnp.float32
BF16 = jnp.bfloat16

HEAD_DIM = 64
CHUNK = 128
SSM_CH = 16
EPS = 1e-6
LANES = 128
BF16_ROWS = 16
NEG = -1e30
LOG2E = 1.4426950408889634
KV_TILES_PER_TRIP = 4

VMEM_LIMIT = 52 * 1024 * 1024
HIGHEST = lax.Precision.HIGHEST
NT_DIMS = (((1,), (1,)), ((), ()))


def _cparams(*sem):
    return pltpu.CompilerParams(dimension_semantics=sem, vmem_limit_bytes=VMEM_LIMIT)


def _rms(x):
    return x * lax.rsqrt(jnp.mean(x * x, axis=-1, keepdims=True) + EPS)


def _log_sigmoid(x):
    return jnp.minimum(x, 0.0) - jnp.log1p(jnp.exp(-jnp.abs(x)))


def _const_spec(shape):
    return pl.BlockSpec(shape, lambda *_: (0,) * len(shape), pipeline_mode=pl.Buffered(1))


def _split3_bf16(x):
    hi = x.astype(BF16).astype(F32)
    r = x - hi
    mid = r.astype(BF16).astype(F32)
    return hi, mid, (r - mid).astype(BF16).astype(F32)


def _proj_in_kernel(*refs, d_att, d_mlp, d_ssm, tm, prompt):
    if prompt:
        (h_ref, g_ref, w1_ref, w2t_ref, bfr_ref, bfc_ref, gqc_ref, gkr_ref, gkc_ref, gavg_ref, ws_ref, bs_ref,
         _kt_prev, _vt_prev, _lf_prev,
         qt_ref, kb_ref, aug_ref, tot_ref, kt_ref, vt_ref, vtb_ref, lf_ref, mlp_ref, su_ref) = refs
    else:
        (h_ref, g_ref, w1_ref, bfr_ref, gqr_ref, gkr_ref, gavg_ref, ws_ref, bs_ref,
         q_ref, k_ref, v_ref, lf_ref, mlp_ref, su_ref, mv_ref) = refs
    n_heads = d_att // HEAD_DIM
    xn = (_rms(h_ref[...]) * g_ref[...]).astype(BF16)
    z = jnp.dot(xn, w1_ref[...], preferred_element_type=F32)
    mu = jax.nn.gelu(z[:, d_att:d_att + d_mlp])
    mv = jax.nn.gelu(z[:, d_att + d_mlp:d_att + 2 * d_mlp])
    o = d_att + 2 * d_mlp + d_ssm
    su_ref[...] = z[:, d_att + 2 * d_mlp:o]
    lane = lax.broadcasted_iota(jnp.int32, (tm, LANES), 1)
    lf_tok = jnp.where(lane < n_heads, _log_sigmoid(z[:, o:o + LANES] + bfr_ref[...]), 0.0)

    gavg = gavg_ref[...]

    def head_norm(t, g):
        ms = jnp.dot((t * t).astype(BF16), gavg, preferred_element_type=F32)
        return t * lax.rsqrt(ms + EPS) * g

    def head_norm_t(t, g):
        t3 = t.reshape(n_heads, HEAD_DIM, tm)
        ms = jnp.mean(t3 * t3, axis=1, keepdims=True)
        return (t3 * lax.rsqrt(ms + EPS)).reshape(d_att, tm) * g

    if prompt:
        kb_ref[...] = head_norm(z[:, :d_att], gkr_ref[...]).astype(BF16)
        row = lax.broadcasted_iota(jnp.int32, (tm, LANES), 0)
        c_loc = lf_tok
        s = 1
        while s < tm:
            c_loc = c_loc + jnp.where(row >= s, pltpu.roll(c_loc, s, 0), 0.0)
            s *= 2
        hi, mid, lo = _split3_bf16(-LOG2E * c_loc)
        aug_ref[...] = (hi + pltpu.roll(mid, n_heads, 1) + pltpu.roll(lo, 2 * n_heads, 1)).astype(BF16)

        f = lax.dot_general(w2t_ref[...], xn, NT_DIMS, preferred_element_type=F32)
        qt_ref[0] = (head_norm_t(f[:d_att], gqc_ref[...]) * (LOG2E * HEAD_DIM ** -0.5)).astype(BF16)
        kt_ref[0, 0] = head_norm_t(f[d_att:2 * d_att], gkc_ref[...])
        vt = f[2 * d_att:3 * d_att]
        vt_ref[0, 0] = vt
        vtb_ref[0] = vt.astype(BF16)
        lf_t = _log_sigmoid(f[3 * d_att:3 * d_att + n_heads] + bfc_ref[...])
        lf_ref[0, 0] = lf_t
        tot_ref[0, 0] = jnp.broadcast_to(LOG2E * jnp.sum(lf_t, axis=1, keepdims=True), (n_heads, LANES))
    else:
        q_ref[...] = (head_norm(z[:, :d_att], gqr_ref[...]) * (HEAD_DIM ** -0.5)).astype(BF16)
        k_ref[...] = head_norm(z[:, o + LANES:o + LANES + d_att], gkr_ref[...])
        v_ref[...] = z[:, o + LANES + d_att:]
        lf_ref[...] = lf_tok.T[:n_heads, :]
        mv_ref[...] = mv

    n_mlp_heads = d_mlp // HEAD_DIM
    rows = lax.broadcasted_iota(jnp.int32, (CHUNK, CHUNK), 0)
    cols = lax.broadcasted_iota(jnp.int32, (CHUNK, CHUNK), 1)
    wst = [jnp.where(rows >= cols, ws_ref[h], 0.0).astype(BF16) for h in range(n_mlp_heads)]
    lane_head = lax.broadcasted_iota(jnp.int32, (CHUNK, d_mlp), 1) // HEAD_DIM
    for c in range(tm // CHUNK):
        sl = slice(c * CHUNK, (c + 1) * CHUNK)
        mvc = mv[sl].astype(BF16)
        mix = bs_ref[...]
        for h in range(n_mlp_heads):
            r = jnp.dot(wst[h], mvc, preferred_element_type=F32)
            mix = mix + jnp.where(lane_head == h, r, 0.0)
        mlp_ref[sl, :] = mu[sl] * mix


def _proj_in(h, consts, *, tm, d_att, d_mlp, d_ssm, seq_len=None, stacks=(), layer=None):
    n, d_model = h.shape
    n_heads = d_att // HEAD_DIM
    prompt = seq_len is not None
    assert n % tm == 0 and tm % CHUNK == 0
    row = lambda width: pl.BlockSpec((tm, width), lambda i: (i, 0))
    sds = jax.ShapeDtypeStruct
    aliases = {}
    if prompt:
        batch, nt = n // seq_len, seq_len // tm
        assert seq_len % tm == 0
        col = lambda height: pl.BlockSpec((1, height, tm), lambda i: (i // nt, 0, i % nt))
        stack = lambda height: pl.BlockSpec((1, 1, height, tm), lambda i: (layer, i // nt, 0, i % nt))
        fm = lambda dt: sds((batch, d_att, seq_len), dt)
        out_shape = [fm(BF16), sds((n, d_att), BF16), sds((n, LANES), BF16), sds((batch, nt, n_heads, LANES), F32),
                     sds(stacks[0].shape, F32), sds(stacks[1].shape, F32), fm(BF16), sds(stacks[2].shape, F32),
                     sds((n, d_mlp), F32), sds((n, d_ssm), F32)]
        out_specs = [col(d_att), row(d_att), row(LANES),
                     pl.BlockSpec((1, 1, n_heads, LANES), lambda i: (i // nt, i % nt, 0, 0)),
                     stack(d_att), stack(d_att), col(d_att), stack(n_heads), row(d_mlp), row(d_ssm)]
        first = 1 + len(consts)
        aliases = {first: 4, first + 1: 5, first + 2: 7}
    else:
        out_shape = [sds((n, d_att), BF16), sds((n, d_att), F32), sds((n, d_att), F32), sds((n_heads, n), F32),
                     sds((n, d_mlp), F32), sds((n, d_ssm), F32), sds((n, d_mlp), F32)]
        out_specs = [row(d_att), row(d_att), row(d_att), pl.BlockSpec((n_heads, tm), lambda i: (0, i)),
                     row(d_mlp), row(d_ssm), row(d_mlp)]
    return pl.pallas_call(
        functools.partial(_proj_in_kernel, d_att=d_att, d_mlp=d_mlp, d_ssm=d_ssm, tm=tm, prompt=prompt),
        grid=(n // tm,),
        in_specs=[row(d_model)] + [_const_spec(c.shape) for c in consts]
        + [pl.BlockSpec(memory_space=pl.ANY)] * len(stacks),
        out_specs=out_specs,
        out_shape=out_shape,
        input_output_aliases=aliases,
        compiler_params=_cparams("parallel"),
        name="proj_in",
    )(h, *consts, *stacks)


def _attn_prompt_kernel(qt_ref, k_ref, aug_ref, vt_ref, tot_ref, o_ref, m_sc, l_sc, acc_sc, s_sc, *, tq, n_heads):
    hp = pl.program_id(1)
    i = pl.program_id(2)
    qt = qt_ref[0].astype(F32)
    row = lax.broadcasted_iota(jnp.int32, (LANES, tq), 0)
    q_aug = []
    for e in range(2):
        h = 2 * hp + e
        own = (row < HEAD_DIM) if e == 0 else (row >= HEAD_DIM)
        ones_rows = (row == h) | (row == n_heads + h) | (row == 2 * n_heads + h)
        q_aug.append(jnp.concatenate([jnp.where(own, qt, 0.0), jnp.where(ones_rows, 1.0, 0.0)], axis=0).astype(BF16))

    m_sc[...] = jnp.full(m_sc.shape, NEG, F32)
    l_sc[...] = jnp.zeros(l_sc.shape, F32)
    acc_sc[...] = jnp.zeros(acc_sc.shape, F32)

    def produce(j, slot):
        k0 = pl.multiple_of(j * tq, tq)
        ka = jnp.concatenate([k_ref[pl.ds(k0, tq), :], aug_ref[pl.ds(k0, tq), :]], axis=1)
        for e in range(2):
            s_sc[slot, e] = jnp.dot(ka, q_aug[e], preferred_element_type=F32)

    ones_rows = jnp.ones((BF16_ROWS, tq), BF16)

    def consume(j, slot, refs, masked):
        k0 = pl.multiple_of(j * tq, tq)
        vt = vt_ref[0, :, pl.ds(k0, tq)]
        new_refs = []
        for e in range(2):
            s = s_sc[slot, e]
            if masked:
                kpos = lax.broadcasted_iota(jnp.int32, (tq, tq), 0)
                qpos = lax.broadcasted_iota(jnp.int32, (tq, tq), 1)
                s = jnp.where(kpos <= qpos, s, NEG)
            ref = jnp.tile(refs[e], (1, tq // LANES))
            m_old = m_sc[e]
            m_new = jnp.maximum(m_old, jnp.max(s, axis=0, keepdims=True) - ref)
            alpha = jnp.exp2(m_old - m_new)
            p = jnp.exp2(s - (m_new + ref)).astype(BF16)
            v_aug = jnp.concatenate([vt[e * HEAD_DIM:(e + 1) * HEAD_DIM, :], ones_rows], axis=0)
            pv = jnp.dot(v_aug, p, preferred_element_type=F32)
            l_sc[e] = alpha * l_sc[e] + pv[HEAD_DIM:HEAD_DIM + 1]
            acc_sc[e] = alpha * acc_sc[e] + pv[:HEAD_DIM]
            m_sc[e] = m_new
            new_refs.append(refs[e] + tot_ref[0, j, pl.ds(2 * hp + e, 1), :])
        return tuple(new_refs)

    produce(0, 0)

    def trip(jj, refs):
        j = KV_TILES_PER_TRIP * jj
        for k in range(KV_TILES_PER_TRIP):
            produce(j + k + 1, (k + 1) % 2)
            refs = consume(j + k, k % 2, refs, False)
        return refs

    zero_ref = jnp.zeros((1, LANES), F32)
    n_trips = i // KV_TILES_PER_TRIP
    refs = lax.fori_loop(0, n_trips, trip, (zero_ref, zero_ref))
    base = n_trips * KV_TILES_PER_TRIP

    for rest in range(KV_TILES_PER_TRIP):
        @pl.when(i - base == rest)
        def _(rest=rest):
            r = refs
            for k in range(rest):
                produce(base + k + 1, (k + 1) % 2)
                r = consume(base + k, k % 2, r, False)
            consume(i, rest % 2, r, True)

    out_t = jnp.concatenate([acc_sc[0] / l_sc[0], acc_sc[1] / l_sc[1]], axis=0)
    o_ref[...] = out_t.T


def _attn_prompt(qt_bf, k_bf, aug_bf, vt_bf, tot, *, tq):
    batch, d_att, seq_len = qt_bf.shape
    n_heads = d_att // HEAD_DIM
    n_pairs = d_att // LANES
    nq = seq_len // tq
    assert seq_len % tq == 0 and tq % LANES == 0 and tot.shape == (batch, nq, n_heads, LANES)
    return pl.pallas_call(
        functools.partial(_attn_prompt_kernel, tq=tq, n_heads=n_heads),
        grid=(batch, n_pairs, nq),
        in_specs=[pl.BlockSpec((1, LANES, tq), lambda b, hp, i: (b, hp, i)),
                  pl.BlockSpec((seq_len, LANES), lambda b, hp, i: (b, hp)),
                  pl.BlockSpec((seq_len, LANES), lambda b, hp, i: (b, 0)),
                  pl.BlockSpec((1, LANES, seq_len), lambda b, hp, i: (b, hp, 0)),
                  pl.BlockSpec((1, nq, n_heads, LANES), lambda b, hp, i: (b, 0, 0, 0))],
        out_specs=pl.BlockSpec((tq, LANES), lambda b, hp, i: (b * nq + i, hp)),
        out_shape=jax.ShapeDtypeStruct((batch * seq_len, d_att), F32),
        scratch_shapes=[pltpu.VMEM((2, 1, tq), F32), pltpu.VMEM((2, 1, tq), F32), pltpu.VMEM((2, HEAD_DIM, tq), F32),
                        pltpu.VMEM((2, 2, tq, tq), F32)],
        compiler_params=_cparams("parallel", "parallel", "parallel"),
        name="attn_prompt",
    )(qt_bf, k_bf, aug_bf, vt_bf, tot)


def _attn_sample_kernel(pt_ref, q_ref, kn_ref, vn_ref, lfn_ref, *refs, pages_per_step, n_new):
    del pt_ref
    pg = pages_per_step
    k_refs, v_refs, lf_refs = refs[:pg], refs[pg:2 * pg], refs[2 * pg:3 * pg]
    o_ref, qbd_sc, m_sc, l_sc, acc_sc, coff_sc = refs[3 * pg:]
    g = pl.program_id(1)
    n_rows, d_att = qbd_sc.shape
    n_heads = d_att // HEAD_DIM
    row_head = lax.broadcasted_iota(jnp.int32, (n_rows, d_att), 0) % n_heads
    lane_head = lax.broadcasted_iota(jnp.int32, (n_rows, d_att), 1) // HEAD_DIM
    own = row_head == lane_head

    @pl.when(g == 0)
    def _init():
        qf = q_ref[0].astype(F32)
        rows = jnp.concatenate([jnp.broadcast_to(qf[i:i + 1], (n_heads, d_att)) for i in range(n_new)], axis=0)
        qbd_sc[...] = jnp.where(own, rows, 0.0).astype(BF16)
        m_sc[...] = jnp.full(m_sc.shape, NEG, F32)
        l_sc[...] = jnp.zeros(l_sc.shape, F32)
        acc_sc[...] = jnp.zeros(acc_sc.shape, F32)
        coff_sc[...] = jnp.zeros(coff_sc.shape, F32)

    lf = jnp.concatenate([lf_refs[p][0, 0] for p in range(pg)], axis=0)
    tri = (lax.broadcasted_iota(jnp.int32, (CHUNK, CHUNK), 0)
           <= lax.broadcasted_iota(jnp.int32, (CHUNK, CHUNK), 1)).astype(BF16)
    c_in = sum(jnp.dot(t.astype(BF16), tri, preferred_element_type=F32) for t in _split3_bf16(lf))
    tot = jnp.broadcast_to(c_in[:, CHUNK - 1:CHUNK], c_in.shape)
    page = lax.broadcasted_iota(jnp.int32, c_in.shape, 0) // n_heads
    incl = tot
    s = 1
    while s < pg:
        incl = incl + jnp.where(page >= s, pltpu.roll(incl, s * n_heads, 0), 0.0)
        s *= 2
    coff = coff_sc[...]
    ck = c_in + (incl - tot) + jnp.concatenate([coff] * pg, axis=0)
    coff_sc[...] = coff + incl[(pg - 1) * n_heads:, :]

    qbd = qbd_sc[...]
    s_parts = []
    for p in range(pg):
        ck_p = jnp.concatenate([ck[p * n_heads:(p + 1) * n_heads]] * n_new, axis=0)
        s_parts.append(jnp.dot(qbd, k_refs[p][0, 0].astype(BF16), preferred_element_type=F32) - ck_p)
    s_all = jnp.concatenate(s_parts, axis=1)
    m_old = m_sc[...]
    m_new = jnp.maximum(m_old, jnp.max(s_all, axis=1, keepdims=True))
    alpha = jnp.exp(m_old - m_new)
    p_all = jnp.exp(s_all - m_new)
    l_sc[...] = alpha * l_sc[...] + jnp.sum(p_all, axis=1, keepdims=True)
    pv = jnp.zeros(acc_sc.shape, F32)
    for p in range(pg):
        pv = pv + lax.dot_general(p_all[:, p * CHUNK:(p + 1) * CHUNK].astype(BF16), v_refs[p][0, 0].astype(BF16),
                                  NT_DIMS, preferred_element_type=F32)
    acc_sc[...] = alpha * acc_sc[...] + pv
    m_sc[...] = m_new

    @pl.when(g == pl.num_programs(1) - 1)
    def _finish():
        qf = qbd_sc[...].astype(F32)
        row_query = lax.broadcasted_iota(jnp.int32, (n_rows, 1), 0) // n_heads
        lfn = jnp.concatenate([lfn_ref[0]] * n_new, axis=0)
        c = jnp.concatenate([coff_sc[...]] * n_new, axis=0)[:, 0:1]
        m, l, acc = m_sc[...], l_sc[...], acc_sc[...]
        for j in range(n_new):
            c = c + lfn[:, j:j + 1]
            s = jnp.sum(qf * kn_ref[0, j:j + 1, :], axis=1, keepdims=True) - c
            s = jnp.where(row_query >= j, s, NEG)
            m_new = jnp.maximum(m, s)
            alpha = jnp.exp(m - m_new)
            pj = jnp.exp(s - m_new)
            l = alpha * l + pj
            acc = alpha * acc + pj * vn_ref[0, j:j + 1, :]
            m = m_new
        out = jnp.where(own, acc / l, 0.0)
        for i in range(n_new):
            o_ref[0, i:i + 1, :] = jnp.sum(out[i * n_heads:(i + 1) * n_heads], axis=0, keepdims=True)


def _attn_sample(page_table, q_bf, k_new, v_new, lf_new, cache_kt, cache_vt, cache_lft, *, layer, pages_per_step):
    n_seq, n_new, d_att = q_bf.shape
    n_heads = d_att // HEAD_DIM
    n_pages = page_table.shape[1]
    pg = pages_per_step
    assert n_pages % pg == 0 and cache_kt.shape[2:] == (d_att, CHUNK) and cache_lft.shape[2:] == (n_heads, CHUNK)
    n_rows = n_new * n_heads

    def page_spec(height, p):
        return pl.BlockSpec((1, 1, height, CHUNK), lambda s, g, pt: (layer, pt[s, g * pg + p], 0, 0))

    seq_spec = lambda a: pl.BlockSpec((1,) + a.shape[1:], lambda s, g, pt: (s, 0, 0))
    grid_spec = pltpu.PrefetchScalarGridSpec(
        num_scalar_prefetch=1,
        grid=(n_seq, n_pages // pg),
        in_specs=[seq_spec(q_bf), seq_spec(k_new), seq_spec(v_new), seq_spec(lf_new)]
        + [page_spec(d_att, p) for p in range(pg)]
        + [page_spec(d_att, p) for p in range(pg)]
        + [page_spec(n_heads, p) for p in range(pg)],
        out_specs=pl.BlockSpec((1, n_new, d_att), lambda s, g, pt: (s, 0, 0)),
        scratch_shapes=[pltpu.VMEM((n_rows, d_att), BF16), pltpu.VMEM((n_rows, 1), F32), pltpu.VMEM((n_rows, 1), F32),
                        pltpu.VMEM((n_rows, d_att), F32), pltpu.VMEM((n_heads, CHUNK), F32)],
    )
    return pl.pallas_call(
        functools.partial(_attn_sample_kernel, pages_per_step=pg, n_new=n_new),
        grid_spec=grid_spec,
        out_shape=jax.ShapeDtypeStruct((n_seq, n_new, d_att), F32),
        compiler_params=_cparams("parallel", "arbitrary"),
        name="attn_sample",
    )(page_table, q_bf, k_new, v_new, lf_new, *([cache_kt] * pg), *([cache_vt] * pg), *([cache_lft] * pg))


def _s5_glu(y, wg_ref, bg_ref):
    z = jax.nn.gelu(y)
    gate = jax.nn.sigmoid(jnp.dot(z.astype(BF16), wg_ref[...], preferred_element_type=F32) + bg_ref[...])
    return z * gate


def _s5_prompt_kernel(u_ref, h0_ref, ab_ref, bb_ref, cc_ref, d_ref, wg_ref, bg_ref, o_ref, hT_ref,
                      h_sc, bu_sc, hs_sc, *, ts, n_state):
    t = pl.program_id(1)

    @pl.when(t == 0)
    def _():
        h_sc[...] = h0_ref[0]

    u = u_ref[...]
    bu_sc[...] = jnp.dot(u.astype(BF16), bb_ref[...], preferred_element_type=F32)
    a_re = ab_ref[0:1, :]
    a_im = ab_ref[1:2, :]

    def body(tt, carry):
        h_re, h_im = carry
        b_re = bu_sc[pl.ds(tt, 1), :n_state]
        b_im = bu_sc[pl.ds(tt, 1), n_state:]
        n_re = a_re * h_re - a_im * h_im + b_re
        n_im = a_re * h_im + a_im * h_re + b_im
        hs_sc[pl.ds(tt, 1), :n_state] = n_re
        hs_sc[pl.ds(tt, 1), n_state:] = n_im
        return n_re, n_im

    h_re, h_im = lax.fori_loop(0, ts, body, (h_sc[:, :n_state], h_sc[:, n_state:]), unroll=8)
    h_sc[:, :n_state] = h_re
    h_sc[:, n_state:] = h_im
    y = jnp.dot(hs_sc[...].astype(BF16), cc_ref[...], preferred_element_type=F32) + d_ref[...] * u
    o_ref[...] = _s5_glu(y, wg_ref, bg_ref)

    @pl.when(t == pl.num_programs(1) - 1)
    def _():
        hT_ref[0] = h_sc[...]


def _s5_prompt(u, h0, ab, bb, cc, d, wg, bg, *, batch, seq_len, ts):
    n, d_ssm = u.shape
    n_state = ab.shape[1]
    nt = seq_len // ts
    assert seq_len % ts == 0
    consts = [ab, bb, cc, d, wg, bg]
    return pl.pallas_call(
        functools.partial(_s5_prompt_kernel, ts=ts, n_state=n_state),
        grid=(batch, nt),
        in_specs=[pl.BlockSpec((ts, d_ssm), lambda b, t: (b * nt + t, 0)),
                  pl.BlockSpec((1, 1, 2 * n_state), lambda b, t: (b, 0, 0))]
        + [_const_spec(c.shape) for c in consts],
        out_specs=[pl.BlockSpec((ts, d_ssm), lambda b, t: (b * nt + t, 0)),
                   pl.BlockSpec((1, 1, 2 * n_state), lambda b, t: (b, 0, 0))],
        out_shape=[jax.ShapeDtypeStruct((n, d_ssm), F32), jax.ShapeDtypeStruct((batch, 1, 2 * n_state), F32)],
        scratch_shapes=[pltpu.VMEM((1, 2 * n_state), F32), pltpu.VMEM((ts, 2 * n_state), F32),
                        pltpu.VMEM((ts, 2 * n_state), F32)],
        compiler_params=_cparams("parallel", "arbitrary"),
        name="s5_prompt",
    )(u, h0, *consts)


def _s5_sample_kernel(u_ref, h0_ref, ab_ref, bb_ref, cc_ref, d_ref, wg_ref, bg_ref, o_ref, hT_ref, *, n_state):
    a_re = ab_ref[0:1, :]
    a_im = ab_ref[1:2, :]
    h_re = h0_ref[:, :n_state]
    h_im = h0_ref[:, n_state:]
    for t in range(u_ref.shape[0]):
        u = u_ref[t]
        bu = jnp.dot(u, bb_ref[...], precision=HIGHEST, preferred_element_type=F32)
        h_re, h_im = (a_re * h_re - a_im * h_im + bu[:, :n_state],
                      a_re * h_im + a_im * h_re + bu[:, n_state:])
        hs = jnp.concatenate([h_re, h_im], axis=1).astype(BF16)
        y = jnp.dot(hs, cc_ref[...], preferred_element_type=F32) + d_ref[...] * u
        o_ref[t] = _s5_glu(y, wg_ref, bg_ref)
    hT_ref[:, :n_state] = h_re
    hT_ref[:, n_state:] = h_im


def _s5_sample(u, h0, ab, bb_f32, cc, d, wg, bg):
    n_new, n_seq, d_ssm = u.shape
    n_state = ab.shape[1]
    return pl.pallas_call(
        functools.partial(_s5_sample_kernel, n_state=n_state),
        out_shape=[jax.ShapeDtypeStruct((n_new, n_seq, d_ssm), F32), jax.ShapeDtypeStruct((n_seq, 2 * n_state), F32)],
        compiler_params=pltpu.CompilerParams(vmem_limit_bytes=VMEM_LIMIT),
        name="s5_sample",
    )(u, h0, ab, bb_f32, cc, d, wg, bg)


def _post_kernel(h_ref, att_ref, mlp_ref, ssm_ref, pe_ref, gmo_ref, wo_ref, gf_ref, wg_ref, wu_ref, wd_ref,
                 gp_ref, wpg_ref, wpp_ref, o_ref, *, d_att, d_mlp):
    gmo = gmo_ref[...]
    e1 = d_att + d_mlp
    a = (_rms(att_ref[...]) * gmo[:, :d_att]).astype(BF16)
    m = (_rms(mlp_ref[...]) * gmo[:, d_att:e1]).astype(BF16)
    s = (_rms(ssm_ref[...]) * gmo[:, e1:]).astype(BF16)
    h = h_ref[...] + (jnp.dot(a, wo_ref[:d_att, :], preferred_element_type=F32)
                      + jnp.dot(m, wo_ref[d_att:e1, :], preferred_element_type=F32)
                      + jnp.dot(s, wo_ref[e1:, :], preferred_element_type=F32))
    hn = (_rms(h) * gf_ref[...]).astype(BF16)
    ff = jax.nn.silu(jnp.dot(hn, wg_ref[...], preferred_element_type=F32)) * jnp.dot(
        hn, wu_ref[...], preferred_element_type=F32)
    h = h + jnp.dot(ff.astype(BF16), wd_ref[...], preferred_element_type=F32)
    gate = jax.nn.sigmoid(jnp.dot((_rms(h) * gp_ref[...]).astype(BF16), wpg_ref[...], preferred_element_type=F32))
    o_ref[...] = h + jnp.dot(pe_ref[0].astype(BF16), wpp_ref[...], preferred_element_type=F32) * gate


def _post(h, att, mlp, ssm, pe, gmo, wo, gf, wg, wu, wd, gp, wpg, wpp, *, tm, pe_layer):
    n, d_model = h.shape
    assert n % tm == 0
    row = lambda a: pl.BlockSpec((tm, a.shape[1]), lambda i: (i, 0))
    consts = [gmo, wo, gf, wg, wu, wd, gp, wpg, wpp]
    return pl.pallas_call(
        functools.partial(_post_kernel, d_att=att.shape[1], d_mlp=mlp.shape[1]),
        grid=(n // tm,),
        in_specs=[row(h), row(att), row(mlp), row(ssm), pl.BlockSpec((1, tm, pe.shape[2]), lambda i: (pe_layer, i, 0))]
        + [_const_spec(c.shape) for c in consts],
        out_specs=row(h),
        out_shape=jax.ShapeDtypeStruct((n, d_model), F32),
        compiler_params=_cparams("parallel"),
        name="post",
    )(h, att, mlp, ssm, pe, *consts)


def _block_diag(blocks):
    g, r, c = blocks.shape
    eye = jnp.eye(g, dtype=blocks.dtype)
    return (eye[:, None, :, None] * blocks[:, :, None, :]).reshape(g * r, g * c)


def _s5_params(a_re, a_im, log_dt, b_re, b_im, c_re, c_im, d, w_glu, b_glu):
    dt = jnp.exp(log_dt)[:, None]
    mag = jnp.exp(a_re * dt)
    ab_re = mag * jnp.cos(a_im * dt)
    ab_im = mag * jnp.sin(a_im * dt)
    den = a_re * a_re + a_im * a_im
    coef_re = ((ab_re - 1.0) * a_re + ab_im * a_im) / den
    coef_im = (ab_im * a_re - (ab_re - 1.0) * a_im) / den
    bb_re = coef_re[..., None] * b_re - coef_im[..., None] * b_im
    bb_im = coef_re[..., None] * b_im + coef_im[..., None] * b_re
    ab = jnp.stack([ab_re.reshape(-1), ab_im.reshape(-1)])
    bb = jnp.concatenate([_block_diag(jnp.swapaxes(bb_re, 1, 2)), _block_diag(jnp.swapaxes(bb_im, 1, 2))], axis=1)
    cc = jnp.concatenate([_block_diag(jnp.swapaxes(c_re, 1, 2)), -_block_diag(jnp.swapaxes(c_im, 1, 2))], axis=0)
    return ab, bb, cc.astype(BF16), d[None, :], _block_diag(w_glu).astype(BF16), b_glu[None, :]


def kernel(x_prompt, x_sample, cache_k, cache_v, cache_logf, state_ssm_re, state_ssm_im, page_table, p_prompt, p_sample, g_norm_mix, w_in, b_forget, g_q, g_k, w_spatial, b_spatial, ssm_a_re, ssm_a_im, ssm_log_dt, ssm_b_re, ssm_b_im, ssm_c_re, ssm_c_im, ssm_d, w_glu, b_glu, g_mix_out, w_out, g_norm_ffn, w_ffn_gate, w_ffn_up, w_ffn_down, g_norm_ple, w_ple_gate, w_ple_proj):
    batch, seq_len, d_model = x_prompt.shape
    n_seq, n_new, _ = x_sample.shape
    depth, n_pool, page, n_heads, head_dim = cache_k.shape
    assert head_dim == HEAD_DIM and page == CHUNK and n_seq * n_new == CHUNK
    d_att = n_heads * HEAD_DIM
    n_mlp_heads = w_spatial.shape[1]
    d_mlp = n_mlp_heads * HEAD_DIM
    n_groups, n_st = ssm_a_re.shape[1:]
    d_ssm = n_groups * SSM_CH
    n_state = n_groups * n_st
    n_p = batch * seq_len
    n_s = n_seq * n_new

    tq = min(512, seq_len)
    ts = min(512, seq_len)
    tm_post = min(256, seq_len)
    pages_per_step = min(16, page_table.shape[1])

    cache_kt = jnp.transpose(cache_k, (0, 1, 3, 4, 2)).reshape(depth, n_pool, d_att, page)
    cache_vt = jnp.transpose(cache_v, (0, 1, 3, 4, 2)).reshape(depth, n_pool, d_att, page)
    cache_lft = jnp.swapaxes(cache_logf, 2, 3)

    gavg = _block_diag(jnp.full((n_heads, HEAD_DIM, HEAD_DIM), 1.0 / HEAD_DIM, F32)).astype(BF16)
    tril_new = jnp.tril(jnp.ones((n_new, n_new), F32))
    eye_seq = jnp.eye(n_seq, dtype=F32)

    hp = x_prompt.reshape(n_p, d_model)
    hs = x_sample.reshape(n_s, d_model)
    h0_prompt = jnp.zeros((batch, 1, 2 * n_state), F32)
    pe_prompt = p_prompt.reshape(depth, n_p, -1)
    pe_sample = p_sample.reshape(depth, n_s, -1)
    outs = {name: [] for name in ("srp", "sip", "ks", "vs", "lfs", "srs", "sis", "mvs")}
    kt_all = jnp.zeros((depth, batch, d_att, seq_len), F32)
    vt_all = jnp.zeros((depth, batch, d_att, seq_len), F32)
    lf_all = jnp.zeros((depth, batch, n_heads, seq_len), F32)

    f0, f1 = 3 * d_att, 3 * d_att + n_heads
    w_q, w_k, w_v, w_f, w_rest = (w_in[:, :, :d_att], w_in[:, :, d_att:2 * d_att], w_in[:, :, 2 * d_att:f0],
                                  w_in[:, :, f0:f1], w_in[:, :, f1:])
    w_f_pad = jnp.concatenate([w_f, jnp.zeros((depth, d_model, LANES - n_heads), F32)], axis=2)
    w1_p_all = jnp.concatenate([w_k, w_rest, w_f_pad], axis=2).astype(BF16)
    pad_rows = -(3 * d_att + n_heads) % BF16_ROWS
    w2t_p_all = jnp.concatenate([jnp.swapaxes(w, 1, 2) for w in (w_q, w_k, w_v, w_f)]
                                + [jnp.zeros((depth, pad_rows, d_model), F32)], axis=1).astype(BF16)
    w1_s_all = jnp.concatenate([w_q, w_rest, w_f_pad, w_k, w_v], axis=2).astype(BF16)
    bf_row_all = jnp.concatenate([b_forget, jnp.zeros((depth, LANES - n_heads), F32)], axis=1)[:, None, :]
    gq_all = jnp.tile(g_q, (1, n_heads))
    gk_all = jnp.tile(g_k, (1, n_heads))
    bs_p_all = jnp.repeat(jnp.swapaxes(b_spatial, 1, 2), HEAD_DIM, axis=2)
    ws_s_all = jnp.einsum("st,lhij->lhsitj", eye_seq, w_spatial[:, :, :n_new, :n_new] * tril_new).reshape(
        depth, n_mlp_heads, n_s, n_s)
    bs_s_all = jnp.tile(jnp.repeat(jnp.swapaxes(b_spatial[:, :, :n_new], 1, 2), HEAD_DIM, axis=2), (1, n_seq, 1))
    s5_all = jax.vmap(_s5_params)(ssm_a_re, ssm_a_im, ssm_log_dt, ssm_b_re, ssm_b_im, ssm_c_re, ssm_c_im,
                                  ssm_d, w_glu, b_glu)
    bb_bf_all = s5_all[1].astype(BF16)
    post_all = (g_mix_out[:, None, :], w_out.astype(BF16), g_norm_ffn[:, None, :], w_ffn_gate.astype(BF16),
                w_ffn_up.astype(BF16), w_ffn_down.astype(BF16), g_norm_ple[:, None, :],
                w_ple_gate.astype(BF16), w_ple_proj.astype(BF16))
    dims = dict(d_att=d_att, d_mlp=d_mlp, d_ssm=d_ssm)

    for i in range(depth):
        w1_p, w2t_p, w1_s, bf_row = w1_p_all[i], w2t_p_all[i], w1_s_all[i], bf_row_all[i]
        gq, gk, bs_p, ws_s, bs_s = gq_all[i], gk_all[i], bs_p_all[i], ws_s_all[i], bs_s_all[i]
        ab, bb, cc, d_row, wglu_bd, bglu = (a[i] for a in s5_all)
        bb_bf = bb_bf_all[i]
        post_w = tuple(a[i] for a in post_all)
        g_mix = g_norm_mix[i][None, :]

        qt_bf, k_bf, aug_bf, tot, kt_all, vt_all, vt_bf, lf_all, mlp, su = _proj_in(
            hp, [g_mix, w1_p, w2t_p, bf_row, b_forget[i][:, None], gq[:, None], gk[None, :], gk[:, None], gavg,
                 w_spatial[i], bs_p], tm=tq, seq_len=seq_len, stacks=(kt_all, vt_all, lf_all), layer=i, **dims)
        att = _attn_prompt(qt_bf, k_bf, aug_bf, vt_bf, tot, tq=tq)
        ssm, h_t = _s5_prompt(su, h0_prompt, ab, bb_bf, cc, d_row, wglu_bd, bglu,
                              batch=batch, seq_len=seq_len, ts=ts)
        hp = _post(hp, att, mlp, ssm, pe_prompt, *post_w, tm=tm_post, pe_layer=i)
        outs["srp"].append(h_t[:, 0, :n_state].reshape(batch, n_groups, n_st))
        outs["sip"].append(h_t[:, 0, n_state:].reshape(batch, n_groups, n_st))

        q_bf, k, v, lf_t, mlp, su, mv = _proj_in(
            hs, [g_mix, w1_s, bf_row, gq[None, :], gk[None, :], gavg, ws_s, bs_s], tm=n_s, **dims)
        lf_new = jnp.swapaxes(lf_t.reshape(n_heads, n_seq, n_new), 0, 1)
        att = _attn_sample(page_table, q_bf.reshape(n_seq, n_new, d_att), k.reshape(n_seq, n_new, d_att),
                           v.reshape(n_seq, n_new, d_att), lf_new, cache_kt, cache_vt, cache_lft,
                           layer=i, pages_per_step=pages_per_step)
        h0_s = jnp.concatenate([state_ssm_re[i].reshape(n_seq, n_state), state_ssm_im[i].reshape(n_seq, n_state)], axis=1)
        ssm, h_t = _s5_sample(jnp.swapaxes(su.reshape(n_seq, n_new, d_ssm), 0, 1), h0_s, ab, bb, cc, d_row, wglu_bd, bglu)
        ssm = jnp.swapaxes(ssm, 0, 1).reshape(n_s, d_ssm)
        hs = _post(hs, att.reshape(n_s, d_att), mlp, ssm, pe_sample, *post_w, tm=n_s, pe_layer=i)
        outs["ks"].append(k.reshape(n_seq, n_new, n_heads, HEAD_DIM))
        outs["vs"].append(v.reshape(n_seq, n_new, n_heads, HEAD_DIM))
        outs["lfs"].append(lf_t.T.reshape(n_seq, n_new, n_heads))
        outs["srs"].append(h_t[:, :n_state].reshape(n_seq, n_groups, n_st))
        outs["sis"].append(h_t[:, n_state:].reshape(n_seq, n_groups, n_st))
        outs["mvs"].append(mv.reshape(n_seq, n_new, d_mlp))

    st = {name: jnp.stack(v) for name, v in outs.items()}
    to_heads = lambda a: jnp.transpose(a.reshape(depth, batch, n_heads, HEAD_DIM, seq_len), (0, 1, 4, 2, 3))
    return (hp.reshape(batch, seq_len, d_model), hs.reshape(n_seq, n_new, d_model),
            to_heads(kt_all), to_heads(vt_all), jnp.swapaxes(lf_all, 2, 3), st["srp"], st["sip"],
            st["ks"], st["vs"], st["lfs"], st["srs"], st["sis"], st["mvs"])
```

```python
import functools

import jax
import jax.numpy as jnp
from jax import lax
from jax.experimental import pallas as pl
from jax.experimental.pallas import tpu as pltpu

F32 = jnp.float32
BF16 = jnp.bfloat16

HEAD_DIM = 64
CHUNK = 128
SSM_CH = 16
EPS = 1e-6
LANES = 128
BF16_ROWS = 16
NEG = -1e30
LOG2E = 1.4426950408889634
KV_TILES_PER_TRIP = 4

VMEM_LIMIT = 52 * 1024 * 1024
HIGHEST = lax.Precision.HIGHEST
NT_DIMS = (((1,), (1,)), ((), ()))


def _cparams(*sem):
    return pltpu.CompilerParams(dimension_semantics=sem, vmem_limit_bytes=VMEM_LIMIT)


def _rms(x):
    return x * lax.rsqrt(jnp.mean(x * x, axis=-1, keepdims=True) + EPS)


def _log_sigmoid(x):
    return jnp.minimum(x, 0.0) - jnp.log1p(jnp.exp(-jnp.abs(x)))


def _const_spec(shape):
    return pl.BlockSpec(shape, lambda *_: (0,) * len(shape), pipeline_mode=pl.Buffered(1))


def _split3_bf16(x):
    hi = x.astype(BF16).astype(F32)
    r = x - hi
    mid = r.astype(BF16).astype(F32)
    return hi, mid, (r - mid).astype(BF16).astype(F32)


def _proj_in_kernel(*refs, d_att, d_mlp, d_ssm, tm, prompt):
    if prompt:
        (h_ref, g_ref, w1_ref, w2t_ref, bfr_ref, bfc_ref, gqc_ref, gkc_ref, ws_ref, bs_ref,
         _kt_prev, _vt_prev, _lf_prev,
         qt_ref, kb_ref, aug_ref, tot_ref, kt_ref, vt_ref, vtb_ref, lf_ref, mlp_ref, su_ref) = refs
    else:
        (h_ref, g_ref, w1_ref, bfr_ref, gqr_ref, gkr_ref, gavg_ref, ws_ref, bs_ref,
         q_ref, k_ref, v_ref, lf_ref, mlp_ref, su_ref, mv_ref) = refs
    n_heads = d_att // HEAD_DIM
    xn = (_rms(h_ref[...]) * g_ref[...]).astype(BF16)
    z = jnp.dot(xn, w1_ref[...], preferred_element_type=F32)
    m0 = 0 if prompt else d_att
    mu = jax.nn.gelu(z[:, m0:m0 + d_mlp])
    mv = jax.nn.gelu(z[:, m0 + d_mlp:m0 + 2 * d_mlp])
    o = m0 + 2 * d_mlp + d_ssm
    su_ref[...] = z[:, m0 + 2 * d_mlp:o]
    lane = lax.broadcasted_iota(jnp.int32, (tm, LANES), 1)
    lf_tok = jnp.where(lane < n_heads, _log_sigmoid(z[:, o:o + LANES] + bfr_ref[...]), 0.0)

    def head_norm(t, g):
        ms = jnp.dot((t * t).astype(BF16), gavg_ref[...], preferred_element_type=F32)
        return t * lax.rsqrt(ms + EPS) * g

    def head_norm_t(t, g):
        t3 = t.reshape(n_heads, HEAD_DIM, tm)
        ms = jnp.mean(t3 * t3, axis=1, keepdims=True)
        return (t3 * lax.rsqrt(ms + EPS)).reshape(d_att, tm) * g

    if prompt:
        row = lax.broadcasted_iota(jnp.int32, (tm, LANES), 0)
        c_loc = lf_tok
        s = 1
        while s < tm:
            c_loc = c_loc + jnp.where(row >= s, pltpu.roll(c_loc, s, 0), 0.0)
            s *= 2
        hi, mid, lo = _split3_bf16(-LOG2E * c_loc)
        aug_ref[...] = (hi + pltpu.roll(mid, n_heads, 1) + pltpu.roll(lo, 2 * n_heads, 1)).astype(BF16)

        f = lax.dot_general(w2t_ref[...], xn, NT_DIMS, preferred_element_type=F32)
        qt_ref[0] = (head_norm_t(f[:d_att], gqc_ref[...]) * (LOG2E * HEAD_DIM ** -0.5)).astype(BF16)
        kn_t = head_norm_t(f[d_att:2 * d_att], gkc_ref[...])
        kt_ref[0, 0] = kn_t
        kb_ref[...] = kn_t.T.astype(BF16)
        vt = f[2 * d_att:3 * d_att]
        vt_ref[0, 0] = vt
        vtb_ref[0] = vt.astype(BF16)
        lf_t = _log_sigmoid(f[3 * d_att:3 * d_att + n_heads] + bfc_ref[...])
        lf_ref[0, 0] = lf_t
        tot_ref[0, 0] = jnp.broadcast_to(LOG2E * jnp.sum(lf_t, axis=1, keepdims=True), (n_heads, LANES))
    else:
        q_ref[...] = (head_norm(z[:, :d_att], gqr_ref[...]) * (HEAD_DIM ** -0.5)).astype(BF16)
        k_ref[...] = head_norm(z[:, o + LANES:o + LANES + d_att], gkr_ref[...])
        v_ref[...] = z[:, o + LANES + d_att:]
        lf_ref[...] = lf_tok.T[:n_heads, :]
        mv_ref[...] = mv

    n_mlp_heads = d_mlp // HEAD_DIM
    rows = lax.broadcasted_iota(jnp.int32, (CHUNK, CHUNK), 0)
    cols = lax.broadcasted_iota(jnp.int32, (CHUNK, CHUNK), 1)
    wst = [jnp.where(rows >= cols, ws_ref[h], 0.0).astype(BF16) for h in range(n_mlp_heads)]
    lane_head = lax.broadcasted_iota(jnp.int32, (CHUNK, d_mlp), 1) // HEAD_DIM
    for c in range(tm // CHUNK):
        sl = slice(c * CHUNK, (c + 1) * CHUNK)
        mvc = mv[sl].astype(BF16)
        mix = bs_ref[...]
        for h in range(n_mlp_heads):
            r = jnp.dot(wst[h], mvc, preferred_element_type=F32)
            mix = mix + jnp.where(lane_head == h, r, 0.0)
        mlp_ref[sl, :] = mu[sl] * mix


def _proj_in(h, consts, *, tm, d_att, d_mlp, d_ssm, seq_len=None, stacks=(), layer=None):
    n, d_model = h.shape
    n_heads = d_att // HEAD_DIM
    prompt = seq_len is not None
    assert n % tm == 0 and tm % CHUNK == 0
    row = lambda width: pl.BlockSpec((tm, width), lambda i: (i, 0))
    sds = jax.ShapeDtypeStruct
    aliases = {}
    if prompt:
        batch, nt = n // seq_len, seq_len // tm
        assert seq_len % tm == 0
        col = lambda height: pl.BlockSpec((1, height, tm), lambda i: (i // nt, 0, i % nt))
        stack = lambda height: pl.BlockSpec((1, 1, height, tm), lambda i: (layer, i // nt, 0, i % nt))
        fm = lambda dt: sds((batch, d_att, seq_len), dt)
        out_shape = [fm(BF16), sds((n, d_att), BF16), sds((n, LANES), BF16), sds((batch, nt, n_heads, LANES), F32),
                     sds(stacks[0].shape, F32), sds(stacks[1].shape, F32), fm(BF16), sds(stacks[2].shape, F32),
                     sds((n, d_mlp), F32), sds((n, d_ssm), F32)]
        out_specs = [col(d_att), row(d_att), row(LANES),
                     pl.BlockSpec((1, 1, n_heads, LANES), lambda i: (i // nt, i % nt, 0, 0)),
                     stack(d_att), stack(d_att), col(d_att), stack(n_heads), row(d_mlp), row(d_ssm)]
        first = 1 + len(consts)
        aliases = {first: 4, first + 1: 5, first + 2: 7}
    else:
        out_shape = [sds((n, d_att), BF16), sds((n, d_att), F32), sds((n, d_att), F32), sds((n_heads, n), F32),
                     sds((n, d_mlp), F32), sds((n, d_ssm), F32), sds((n, d_mlp), F32)]
        out_specs = [row(d_att), row(d_att), row(d_att), pl.BlockSpec((n_heads, tm), lambda i: (0, i)),
                     row(d_mlp), row(d_ssm), row(d_mlp)]
    return pl.pallas_call(
        functools.partial(_proj_in_kernel, d_att=d_att, d_mlp=d_mlp, d_ssm=d_ssm, tm=tm, prompt=prompt),
        grid=(n // tm,),
        in_specs=[row(d_model)] + [_const_spec(c.shape) for c in consts]
        + [pl.BlockSpec(memory_space=pl.ANY)] * len(stacks),
        out_specs=out_specs,
        out_shape=out_shape,
        input_output_aliases=aliases,
        compiler_params=_cparams("parallel"),
        name="proj_in",
    )(h, *consts, *stacks)


def _attn_prompt_kernel(qt_ref, k_ref, aug_ref, vt_ref, tot_ref, o_ref, m_sc, l_sc, acc_sc, s_sc, *, tq, n_heads):
    hp = pl.program_id(1)
    i = pl.program_id(2)
    qt = qt_ref[0].astype(F32)
    row = lax.broadcasted_iota(jnp.int32, (LANES, tq), 0)
    q_aug = []
    for e in range(2):
        h = 2 * hp + e
        own = (row < HEAD_DIM) if e == 0 else (row >= HEAD_DIM)
        ones_rows = (row == h) | (row == n_heads + h) | (row == 2 * n_heads + h)
        q_aug.append(jnp.concatenate([jnp.where(own, qt, 0.0), jnp.where(ones_rows, 1.0, 0.0)], axis=0).astype(BF16))

    m_sc[...] = jnp.full(m_sc.shape, NEG, F32)
    l_sc[...] = jnp.zeros(l_sc.shape, F32)
    acc_sc[...] = jnp.zeros(acc_sc.shape, F32)

    def produce(j, slot):
        k0 = pl.multiple_of(j * tq, tq)
        ka = jnp.concatenate([k_ref[pl.ds(k0, tq), :], aug_ref[pl.ds(k0, tq), :]], axis=1)
        for e in range(2):
            s_sc[slot, e] = jnp.dot(ka, q_aug[e], preferred_element_type=F32)

    ones_rows = jnp.ones((BF16_ROWS, tq), BF16)

    def consume(j, slot, refs, masked):
        k0 = pl.multiple_of(j * tq, tq)
        vt = vt_ref[0, :, pl.ds(k0, tq)]
        new_refs = []
        for e in range(2):
            s = s_sc[slot, e]
            if masked:
                kpos = lax.broadcasted_iota(jnp.int32, (tq, tq), 0)
                qpos = lax.broadcasted_iota(jnp.int32, (tq, tq), 1)
                s = jnp.where(kpos <= qpos, s, NEG)
            ref = jnp.tile(refs[e], (1, tq // LANES))
            m_old = m_sc[e]
            m_new = jnp.maximum(m_old, jnp.max(s, axis=0, keepdims=True) - ref)
            alpha = jnp.exp2(m_old - m_new)
            p = jnp.exp2(s - (m_new + ref)).astype(BF16)
            v_aug = jnp.concatenate([vt[e * HEAD_DIM:(e + 1) * HEAD_DIM, :], ones_rows], axis=0)
            pv = jnp.dot(v_aug, p, preferred_element_type=F32)
            l_sc[e] = alpha * l_sc[e] + pv[HEAD_DIM:HEAD_DIM + 1]
            acc_sc[e] = alpha * acc_sc[e] + pv[:HEAD_DIM]
            m_sc[e] = m_new
            new_refs.append(refs[e] + tot_ref[0, j, pl.ds(2 * hp + e, 1), :])
        return tuple(new_refs)

    produce(0, 0)

    def trip(jj, refs):
        j = KV_TILES_PER_TRIP * jj
        for k in range(KV_TILES_PER_TRIP):
            produce(j + k + 1, (k + 1) % 2)
            refs = consume(j + k, k % 2, refs, False)
        return refs

    zero_ref = jnp.zeros((1, LANES), F32)
    n_trips = i // KV_TILES_PER_TRIP
    refs = lax.fori_loop(0, n_trips, trip, (zero_ref, zero_ref))
    base = n_trips * KV_TILES_PER_TRIP

    for rest in range(KV_TILES_PER_TRIP):
        @pl.when(i - base == rest)
        def _(rest=rest):
            r = refs
            for k in range(rest):
                produce(base + k + 1, (k + 1) % 2)
                r = consume(base + k, k % 2, r, False)
            consume(i, rest % 2, r, True)

    out_t = jnp.concatenate([acc_sc[0] / l_sc[0], acc_sc[1] / l_sc[1]], axis=0)
    o_ref[...] = out_t.T


def _attn_prompt(qt_bf, k_bf, aug_bf, vt_bf, tot, *, tq):
    batch, d_att, seq_len = qt_bf.shape
    n_heads = d_att // HEAD_DIM
    n_pairs = d_att // LANES
    nq = seq_len // tq
    assert seq_len % tq == 0 and tq % LANES == 0 and tot.shape == (batch, nq, n_heads, LANES)
    return pl.pallas_call(
        functools.partial(_attn_prompt_kernel, tq=tq, n_heads=n_heads),
        grid=(batch, n_pairs, nq),
        in_specs=[pl.BlockSpec((1, LANES, tq), lambda b, hp, i: (b, hp, i)),
                  pl.BlockSpec((seq_len, LANES), lambda b, hp, i: (b, hp)),
                  pl.BlockSpec((seq_len, LANES), lambda b, hp, i: (b, 0)),
                  pl.BlockSpec((1, LANES, seq_len), lambda b, hp, i: (b, hp, 0)),
                  pl.BlockSpec((1, nq, n_heads, LANES), lambda b, hp, i: (b, 0, 0, 0))],
        out_specs=pl.BlockSpec((tq, LANES), lambda b, hp, i: (b * nq + i, hp)),
        out_shape=jax.ShapeDtypeStruct((batch * seq_len, d_att), F32),
        scratch_shapes=[pltpu.VMEM((2, 1, tq), F32), pltpu.VMEM((2, 1, tq), F32), pltpu.VMEM((2, HEAD_DIM, tq), F32),
                        pltpu.VMEM((2, 2, tq, tq), F32)],
        compiler_params=_cparams("parallel", "parallel", "parallel"),
        name="attn_prompt",
    )(qt_bf, k_bf, aug_bf, vt_bf, tot)


def _attn_sample_kernel(pt_ref, q_ref, kn_ref, vn_ref, lfn_ref, *refs, pages_per_step, n_new):
    del pt_ref
    pg = pages_per_step
    k_refs, v_refs, lf_refs = refs[:pg], refs[pg:2 * pg], refs[2 * pg:3 * pg]
    o_ref, qbd_sc, m_sc, l_sc, acc_sc, coff_sc = refs[3 * pg:]
    g = pl.program_id(1)
    n_rows, d_att = qbd_sc.shape
    n_heads = d_att // HEAD_DIM
    row_head = lax.broadcasted_iota(jnp.int32, (n_rows, d_att), 0) % n_heads
    lane_head = lax.broadcasted_iota(jnp.int32, (n_rows, d_att), 1) // HEAD_DIM
    own = row_head == lane_head

    @pl.when(g == 0)
    def _init():
        qf = q_ref[0].astype(F32)
        rows = jnp.concatenate([jnp.broadcast_to(qf[i:i + 1], (n_heads, d_att)) for i in range(n_new)], axis=0)
        qbd_sc[...] = jnp.where(own, rows, 0.0).astype(BF16)
        m_sc[...] = jnp.full(m_sc.shape, NEG, F32)
        l_sc[...] = jnp.zeros(l_sc.shape, F32)
        acc_sc[...] = jnp.zeros(acc_sc.shape, F32)
        coff_sc[...] = jnp.zeros(coff_sc.shape, F32)

    lf = jnp.concatenate([lf_refs[p][0, 0] for p in range(pg)], axis=0)
    tri = (lax.broadcasted_iota(jnp.int32, (CHUNK, CHUNK), 0)
           <= lax.broadcasted_iota(jnp.int32, (CHUNK, CHUNK), 1)).astype(BF16)
    c_in = sum(jnp.dot(t.astype(BF16), tri, preferred_element_type=F32) for t in _split3_bf16(lf))
    tot = jnp.broadcast_to(c_in[:, CHUNK - 1:CHUNK], c_in.shape)
    page = lax.broadcasted_iota(jnp.int32, c_in.shape, 0) // n_heads
    incl = tot
    s = 1
    while s < pg:
        incl = incl + jnp.where(page >= s, pltpu.roll(incl, s * n_heads, 0), 0.0)
        s *= 2
    coff = coff_sc[...]
    ck = c_in + (incl - tot) + jnp.concatenate([coff] * pg, axis=0)
    coff_sc[...] = coff + incl[(pg - 1) * n_heads:, :]

    qbd = qbd_sc[...]
    s_parts = []
    for p in range(pg):
        ck_p = jnp.concatenate([ck[p * n_heads:(p + 1) * n_heads]] * n_new, axis=0)
        s_parts.append(jnp.dot(qbd, k_refs[p][0, 0].astype(BF16), preferred_element_type=F32) - ck_p)
    s_all = jnp.concatenate(s_parts, axis=1)
    m_old = m_sc[...]
    m_new = jnp.maximum(m_old, jnp.max(s_all, axis=1, keepdims=True))
    alpha = jnp.exp(m_old - m_new)
    p_all = jnp.exp(s_all - m_new)
    l_sc[...] = alpha * l_sc[...] + jnp.sum(p_all, axis=1, keepdims=True)
    pv = jnp.zeros(acc_sc.shape, F32)
    for p in range(pg):
        pv = pv + lax.dot_general(p_all[:, p * CHUNK:(p + 1) * CHUNK].astype(BF16), v_refs[p][0, 0].astype(BF16),
                                  NT_DIMS, preferred_element_type=F32)
    acc_sc[...] = alpha * acc_sc[...] + pv
    m_sc[...] = m_new

    @pl.when(g == pl.num_programs(1) - 1)
    def _finish():
        qf = qbd_sc[...].astype(F32)
        row_query = lax.broadcasted_iota(jnp.int32, (n_rows, 1), 0) // n_heads
        lfn = jnp.concatenate([lfn_ref[0]] * n_new, axis=0)
        c = jnp.concatenate([coff_sc[...]] * n_new, axis=0)[:, 0:1]
        m, l, acc = m_sc[...], l_sc[...], acc_sc[...]
        for j in range(n_new):
            c = c + lfn[:, j:j + 1]
            s = jnp.sum(qf * kn_ref[0, j:j + 1, :], axis=1, keepdims=True) - c
            s = jnp.where(row_query >= j, s, NEG)
            m_new = jnp.maximum(m, s)
            alpha = jnp.exp(m - m_new)
            pj = jnp.exp(s - m_new)
            l = alpha * l + pj
            acc = alpha * acc + pj * vn_ref[0, j:j + 1, :]
            m = m_new
        out = jnp.where(own, acc / l, 0.0)
        for i in range(n_new):
            o_ref[0, i:i + 1, :] = jnp.sum(out[i * n_heads:(i + 1) * n_heads], axis=0, keepdims=True)


def _attn_sample(page_table, q_bf, k_new, v_new, lf_new, cache_kt, cache_vt, cache_lft, *, layer, pages_per_step):
    n_seq, n_new, d_att = q_bf.shape
    n_heads = d_att // HEAD_DIM
    n_pages = page_table.shape[1]
    pg = pages_per_step
    assert n_pages % pg == 0 and cache_kt.shape[2:] == (d_att, CHUNK) and cache_lft.shape[2:] == (n_heads, CHUNK)
    n_rows = n_new * n_heads

    def page_spec(height, p):
        return pl.BlockSpec((1, 1, height, CHUNK), lambda s, g, pt: (layer, pt[s, g * pg + p], 0, 0))

    seq_spec = lambda a: pl.BlockSpec((1,) + a.shape[1:], lambda s, g, pt: (s, 0, 0))
    grid_spec = pltpu.PrefetchScalarGridSpec(
        num_scalar_prefetch=1,
        grid=(n_seq, n_pages // pg),
        in_specs=[seq_spec(q_bf), seq_spec(k_new), seq_spec(v_new), seq_spec(lf_new)]
        + [page_spec(d_att, p) for p in range(pg)]
        + [page_spec(d_att, p) for p in range(pg)]
        + [page_spec(n_heads, p) for p in range(pg)],
        out_specs=pl.BlockSpec((1, n_new, d_att), lambda s, g, pt: (s, 0, 0)),
        scratch_shapes=[pltpu.VMEM((n_rows, d_att), BF16), pltpu.VMEM((n_rows, 1), F32), pltpu.VMEM((n_rows, 1), F32),
                        pltpu.VMEM((n_rows, d_att), F32), pltpu.VMEM((n_heads, CHUNK), F32)],
    )
    return pl.pallas_call(
        functools.partial(_attn_sample_kernel, pages_per_step=pg, n_new=n_new),
        grid_spec=grid_spec,
        out_shape=jax.ShapeDtypeStruct((n_seq, n_new, d_att), F32),
        compiler_params=_cparams("parallel", "arbitrary"),
        name="attn_sample",
    )(page_table, q_bf, k_new, v_new, lf_new, *([cache_kt] * pg), *([cache_vt] * pg), *([cache_lft] * pg))


def _s5_glu(y, wg_ref, bg_ref):
    z = jax.nn.gelu(y)
    gate = jax.nn.sigmoid(jnp.dot(z.astype(BF16), wg_ref[...], preferred_element_type=F32) + bg_ref[...])
    return z * gate


def _s5_prompt_kernel(u_ref, h0_ref, ab_ref, bb_ref, cc_ref, d_ref, wg_ref, bg_ref, o_ref, hT_ref,
                      h_sc, bu_sc, hs_sc, *, ts, n_state):
    t = pl.program_id(1)

    @pl.when(t == 0)
    def _():
        h_sc[...] = h0_ref[0]

    u = u_ref[...]
    bu_sc[...] = jnp.dot(u.astype(BF16), bb_ref[...], preferred_element_type=F32)
    a_re = ab_ref[0:1, :]
    a_im = ab_ref[1:2, :]

    def body(tt, carry):
        h_re, h_im = carry
        b_re = bu_sc[pl.ds(tt, 1), :n_state]
        b_im = bu_sc[pl.ds(tt, 1), n_state:]
        n_re = a_re * h_re - a_im * h_im + b_re
        n_im = a_re * h_im + a_im * h_re + b_im
        hs_sc[pl.ds(tt, 1), :n_state] = n_re
        hs_sc[pl.ds(tt, 1), n_state:] = n_im
        return n_re, n_im

    h_re, h_im = lax.fori_loop(0, ts, body, (h_sc[:, :n_state], h_sc[:, n_state:]), unroll=8)
    h_sc[:, :n_state] = h_re
    h_sc[:, n_state:] = h_im
    y = jnp.dot(hs_sc[...].astype(BF16), cc_ref[...], preferred_element_type=F32) + d_ref[...] * u
    o_ref[...] = _s5_glu(y, wg_ref, bg_ref)

    @pl.when(t == pl.num_programs(1) - 1)
    def _():
        hT_ref[0] = h_sc[...]


def _s5_prompt(u, h0, ab, bb, cc, d, wg, bg, *, batch, seq_len, ts):
    n, d_ssm = u.shape
    n_state = ab.shape[1]
    nt = seq_len // ts
    assert seq_len % ts == 0
    consts = [ab, bb, cc, d, wg, bg]
    return pl.pallas_call(
        functools.partial(_s5_prompt_kernel, ts=ts, n_state=n_state),
        grid=(batch, nt),
        in_specs=[pl.BlockSpec((ts, d_ssm), lambda b, t: (b * nt + t, 0)),
                  pl.BlockSpec((1, 1, 2 * n_state), lambda b, t: (b, 0, 0))]
        + [_const_spec(c.shape) for c in consts],
        out_specs=[pl.BlockSpec((ts, d_ssm), lambda b, t: (b * nt + t, 0)),
                   pl.BlockSpec((1, 1, 2 * n_state), lambda b, t: (b, 0, 0))],
        out_shape=[jax.ShapeDtypeStruct((n, d_ssm), F32), jax.ShapeDtypeStruct((batch, 1, 2 * n_state), F32)],
        scratch_shapes=[pltpu.VMEM((1, 2 * n_state), F32), pltpu.VMEM((ts, 2 * n_state), F32),
                        pltpu.VMEM((ts, 2 * n_state), F32)],
        compiler_params=_cparams("parallel", "arbitrary"),
        name="s5_prompt",
    )(u, h0, *consts)


def _s5_sample_kernel(u_ref, h0_ref, ab_ref, bb_ref, cc_ref, d_ref, wg_ref, bg_ref, o_ref, hT_ref, *, n_state):
    a_re = ab_ref[0:1, :]
    a_im = ab_ref[1:2, :]
    h_re = h0_ref[:, :n_state]
    h_im = h0_ref[:, n_state:]
    for t in range(u_ref.shape[0]):
        u = u_ref[t]
        bu = jnp.dot(u, bb_ref[...], precision=HIGHEST, preferred_element_type=F32)
        h_re, h_im = (a_re * h_re - a_im * h_im + bu[:, :n_state],
                      a_re * h_im + a_im * h_re + bu[:, n_state:])
        hs = jnp.concatenate([h_re, h_im], axis=1).astype(BF16)
        y = jnp.dot(hs, cc_ref[...], preferred_element_type=F32) + d_ref[...] * u
        o_ref[t] = _s5_glu(y, wg_ref, bg_ref)
    hT_ref[:, :n_state] = h_re
    hT_ref[:, n_state:] = h_im


def _s5_sample(u, h0, ab, bb_f32, cc, d, wg, bg):
    n_new, n_seq, d_ssm = u.shape
    n_state = ab.shape[1]
    return pl.pallas_call(
        functools.partial(_s5_sample_kernel, n_state=n_state),
        out_shape=[jax.ShapeDtypeStruct((n_new, n_seq, d_ssm), F32), jax.ShapeDtypeStruct((n_seq, 2 * n_state), F32)],
        compiler_params=pltpu.CompilerParams(vmem_limit_bytes=VMEM_LIMIT),
        name="s5_sample",
    )(u, h0, ab, bb_f32, cc, d, wg, bg)


def _post_kernel(h_ref, att_ref, mlp_ref, ssm_ref, pe_ref, gmo_ref, wo_ref, gf_ref, wg_ref, wu_ref, wd_ref,
                 gp_ref, wpg_ref, wpp_ref, o_ref, *, d_att, d_mlp):
    gmo = gmo_ref[...]
    e1 = d_att + d_mlp
    a = (_rms(att_ref[...]) * gmo[:, :d_att]).astype(BF16)
    m = (_rms(mlp_ref[...]) * gmo[:, d_att:e1]).astype(BF16)
    s = (_rms(ssm_ref[...]) * gmo[:, e1:]).astype(BF16)
    h = h_ref[...] + (jnp.dot(a, wo_ref[:d_att, :], preferred_element_type=F32)
                      + jnp.dot(m, wo_ref[d_att:e1, :], preferred_element_type=F32)
                      + jnp.dot(s, wo_ref[e1:, :], preferred_element_type=F32))
    hn = (_rms(h) * gf_ref[...]).astype(BF16)
    ff = jax.nn.silu(jnp.dot(hn, wg_ref[...], preferred_element_type=F32)) * jnp.dot(
        hn, wu_ref[...], preferred_element_type=F32)
    h = h + jnp.dot(ff.astype(BF16), wd_ref[...], preferred_element_type=F32)
    gate = jax.nn.sigmoid(jnp.dot((_rms(h) * gp_ref[...]).astype(BF16), wpg_ref[...], preferred_element_type=F32))
    o_ref[...] = h + jnp.dot(pe_ref[0].astype(BF16), wpp_ref[...], preferred_element_type=F32) * gate


def _post(h, att, mlp, ssm, pe, gmo, wo, gf, wg, wu, wd, gp, wpg, wpp, *, tm, pe_layer):
    n, d_model = h.shape
    assert n % tm == 0
    row = lambda a: pl.BlockSpec((tm, a.shape[1]), lambda i: (i, 0))
    consts = [gmo, wo, gf, wg, wu, wd, gp, wpg, wpp]
    return pl.pallas_call(
        functools.partial(_post_kernel, d_att=att.shape[1], d_mlp=mlp.shape[1]),
        grid=(n // tm,),
        in_specs=[row(h), row(att), row(mlp), row(ssm), pl.BlockSpec((1, tm, pe.shape[2]), lambda i: (pe_layer, i, 0))]
        + [_const_spec(c.shape) for c in consts],
        out_specs=row(h),
        out_shape=jax.ShapeDtypeStruct((n, d_model), F32),
        compiler_params=_cparams("parallel"),
        name="post",
    )(h, att, mlp, ssm, pe, *consts)


def _block_diag(blocks):
    g, r, c = blocks.shape
    eye = jnp.eye(g, dtype=blocks.dtype)
    return (eye[:, None, :, None] * blocks[:, :, None, :]).reshape(g * r, g * c)


def _s5_params(a_re, a_im, log_dt, b_re, b_im, c_re, c_im, d, w_glu, b_glu):
    dt = jnp.exp(log_dt)[:, None]
    mag = jnp.exp(a_re * dt)
    ab_re = mag * jnp.cos(a_im * dt)
    ab_im = mag * jnp.sin(a_im * dt)
    den = a_re * a_re + a_im * a_im
    coef_re = ((ab_re - 1.0) * a_re + ab_im * a_im) / den
    coef_im = (ab_im * a_re - (ab_re - 1.0) * a_im) / den
    bb_re = coef_re[..., None] * b_re - coef_im[..., None] * b_im
    bb_im = coef_re[..., None] * b_im + coef_im[..., None] * b_re
    ab = jnp.stack([ab_re.reshape(-1), ab_im.reshape(-1)])
    bb = jnp.concatenate([_block_diag(jnp.swapaxes(bb_re, 1, 2)), _block_diag(jnp.swapaxes(bb_im, 1, 2))], axis=1)
    cc = jnp.concatenate([_block_diag(jnp.swapaxes(c_re, 1, 2)), -_block_diag(jnp.swapaxes(c_im, 1, 2))], axis=0)
    return ab, bb, cc.astype(BF16), d[None, :], _block_diag(w_glu).astype(BF16), b_glu[None, :]


def kernel(x_prompt, x_sample, cache_k, cache_v, cache_logf, state_ssm_re, state_ssm_im, page_table, p_prompt, p_sample, g_norm_mix, w_in, b_forget, g_q, g_k, w_spatial, b_spatial, ssm_a_re, ssm_a_im, ssm_log_dt, ssm_b_re, ssm_b_im, ssm_c_re, ssm_c_im, ssm_d, w_glu, b_glu, g_mix_out, w_out, g_norm_ffn, w_ffn_gate, w_ffn_up, w_ffn_down, g_norm_ple, w_ple_gate, w_ple_proj):
    batch, seq_len, d_model = x_prompt.shape
    n_seq, n_new, _ = x_sample.shape
    depth, n_pool, page, n_heads, head_dim = cache_k.shape
    assert head_dim == HEAD_DIM and page == CHUNK and n_seq * n_new == CHUNK
    d_att = n_heads * HEAD_DIM
    n_mlp_heads = w_spatial.shape[1]
    d_mlp = n_mlp_heads * HEAD_DIM
    n_groups, n_st = ssm_a_re.shape[1:]
    d_ssm = n_groups * SSM_CH
    n_state = n_groups * n_st
    n_p = batch * seq_len
    n_s = n_seq * n_new

    tq = min(512, seq_len)
    ts = min(512, seq_len)
    tm_post = min(512, seq_len)
    pages_per_step = min(16, page_table.shape[1])

    cache_kt = jnp.transpose(cache_k, (0, 1, 3, 4, 2)).reshape(depth, n_pool, d_att, page)
    cache_vt = jnp.transpose(cache_v, (0, 1, 3, 4, 2)).reshape(depth, n_pool, d_att, page)
    cache_lft = jnp.swapaxes(cache_logf, 2, 3)

    gavg = _block_diag(jnp.full((n_heads, HEAD_DIM, HEAD_DIM), 1.0 / HEAD_DIM, F32)).astype(BF16)
    tril_new = jnp.tril(jnp.ones((n_new, n_new), F32))
    eye_seq = jnp.eye(n_seq, dtype=F32)

    hp = x_prompt.reshape(n_p, d_model)
    hs = x_sample.reshape(n_s, d_model)
    h0_prompt = jnp.zeros((batch, 1, 2 * n_state), F32)
    pe_prompt = p_prompt.reshape(depth, n_p, -1)
    pe_sample = p_sample.reshape(depth, n_s, -1)
    outs = {name: [] for name in ("srp", "sip", "ks", "vs", "lfs", "srs", "sis", "mvs")}
    kt_all = jnp.zeros((depth, batch, d_att, seq_len), F32)
    vt_all = jnp.zeros((depth, batch, d_att, seq_len), F32)
    lf_all = jnp.zeros((depth, batch, n_heads, seq_len), F32)

    f0, f1 = 3 * d_att, 3 * d_att + n_heads
    w_q, w_k, w_v, w_f, w_rest = (w_in[:, :, :d_att], w_in[:, :, d_att:2 * d_att], w_in[:, :, 2 * d_att:f0],
                                  w_in[:, :, f0:f1], w_in[:, :, f1:])
    w_f_pad = jnp.concatenate([w_f, jnp.zeros((depth, d_model, LANES - n_heads), F32)], axis=2)
    w1_p_all = jnp.concatenate([w_rest, w_f_pad], axis=2).astype(BF16)
    pad_rows = -(3 * d_att + n_heads) % BF16_ROWS
    w2t_p_all = jnp.concatenate([jnp.swapaxes(w, 1, 2) for w in (w_q, w_k, w_v, w_f)]
                                + [jnp.zeros((depth, pad_rows, d_model), F32)], axis=1).astype(BF16)
    w1_s_all = jnp.concatenate([w_q, w_rest, w_f_pad, w_k, w_v], axis=2).astype(BF16)
    bf_row_all = jnp.concatenate([b_forget, jnp.zeros((depth, LANES - n_heads), F32)], axis=1)[:, None, :]
    gq_all = jnp.tile(g_q, (1, n_heads))
    gk_all = jnp.tile(g_k, (1, n_heads))
    bs_p_all = jnp.repeat(jnp.swapaxes(b_spatial, 1, 2), HEAD_DIM, axis=2)
    ws_s_all = jnp.einsum("st,lhij->lhsitj", eye_seq, w_spatial[:, :, :n_new, :n_new] * tril_new).reshape(
        depth, n_mlp_heads, n_s, n_s)
    bs_s_all = jnp.tile(jnp.repeat(jnp.swapaxes(b_spatial[:, :, :n_new], 1, 2), HEAD_DIM, axis=2), (1, n_seq, 1))
    s5_all = jax.vmap(_s5_params)(ssm_a_re, ssm_a_im, ssm_log_dt, ssm_b_re, ssm_b_im, ssm_c_re, ssm_c_im,
                                  ssm_d, w_glu, b_glu)
    bb_bf_all = s5_all[1].astype(BF16)
    post_all = (g_mix_out[:, None, :], w_out.astype(BF16), g_norm_ffn[:, None, :], w_ffn_gate.astype(BF16),
                w_ffn_up.astype(BF16), w_ffn_down.astype(BF16), g_norm_ple[:, None, :],
                w_ple_gate.astype(BF16), w_ple_proj.astype(BF16))
    dims = dict(d_att=d_att, d_mlp=d_mlp, d_ssm=d_ssm)

    for i in range(depth):
        w1_p, w2t_p, w1_s, bf_row = w1_p_all[i], w2t_p_all[i], w1_s_all[i], bf_row_all[i]
        gq, gk, bs_p, ws_s, bs_s = gq_all[i], gk_all[i], bs_p_all[i], ws_s_all[i], bs_s_all[i]
        ab, bb, cc, d_row, wglu_bd, bglu = (a[i] for a in s5_all)
        bb_bf = bb_bf_all[i]
        post_w = tuple(a[i] for a in post_all)
        g_mix = g_norm_mix[i][None, :]

        qt_bf, k_bf, aug_bf, tot, kt_all, vt_all, vt_bf, lf_all, mlp, su = _proj_in(
            hp, [g_mix, w1_p, w2t_p, bf_row, b_forget[i][:, None], gq[:, None], gk[:, None], w_spatial[i], bs_p],
            tm=tq, seq_len=seq_len, stacks=(kt_all, vt_all, lf_all), layer=i, **dims)
        att = _attn_prompt(qt_bf, k_bf, aug_bf, vt_bf, tot, tq=tq)
        ssm, h_t = _s5_prompt(su, h0_prompt, ab, bb_bf, cc, d_row, wglu_bd, bglu,
                              batch=batch, seq_len=seq_len, ts=ts)
        hp = _post(hp, att, mlp, ssm, pe_prompt, *post_w, tm=tm_post, pe_layer=i)
        outs["srp"].append(h_t[:, 0, :n_state].reshape(batch, n_groups, n_st))
        outs["sip"].append(h_t[:, 0, n_state:].reshape(batch, n_groups, n_st))

        q_bf, k, v, lf_t, mlp, su, mv = _proj_in(
            hs, [g_mix, w1_s, bf_row, gq[None, :], gk[None, :], gavg, ws_s, bs_s], tm=n_s, **dims)
        lf_new = jnp.swapaxes(lf_t.reshape(n_heads, n_seq, n_new), 0, 1)
        att = _attn_sample(page_table, q_bf.reshape(n_seq, n_new, d_att), k.reshape(n_seq, n_new, d_att),
                           v.reshape(n_seq, n_new, d_att), lf_new, cache_kt, cache_vt, cache_lft,
                           layer=i, pages_per_step=pages_per_step)
        h0_s = jnp.concatenate([state_ssm_re[i].reshape(n_seq, n_state), state_ssm_im[i].reshape(n_seq, n_state)], axis=1)
        ssm, h_t = _s5_sample(jnp.swapaxes(su.reshape(n_seq, n_new, d_ssm), 0, 1), h0_s, ab, bb, cc, d_row, wglu_bd, bglu)
        ssm = jnp.swapaxes(ssm, 0, 1).reshape(n_s, d_ssm)
        hs = _post(hs, att.reshape(n_s, d_att), mlp, ssm, pe_sample, *post_w, tm=n_s, pe_layer=i)
        outs["ks"].append(k.reshape(n_seq, n_new, n_heads, HEAD_DIM))
        outs["vs"].append(v.reshape(n_seq, n_new, n_heads, HEAD_DIM))
        outs["lfs"].append(lf_t.T.reshape(n_seq, n_new, n_heads))
        outs["srs"].append(h_t[:, :n_state].reshape(n_seq, n_groups, n_st))
        outs["sis"].append(h_t[:, n_state:].reshape(n_seq, n_groups, n_st))
        outs["mvs"].append(mv.reshape(n_seq, n_new, d_mlp))

    st = {name: jnp.stack(v) for name, v in outs.items()}
    to_heads = lambda a: jnp.transpose(a.reshape(depth, batch, n_heads, HEAD_DIM, seq_len), (0, 1, 4, 2, 3))
    return (hp.reshape(batch, seq_len, d_model), hs.reshape(n_seq, n_new, d_model),
            to_heads(kt_all), to_heads(vt_all), jnp.swapaxes(lf_all, 2, 3), st["srp"], st["sip"],
            st["ks"], st["vs"], st["lfs"], st["srs"], st["sis"], st["mvs"])
```

```python
import functools

import jax
import jax.numpy as jnp
from jax import lax
from jax.experimental import pallas as pl
from jax.experimental.pallas import tpu as pltpu

F32 = jnp.float32
BF16 = jnp.bfloat16

HEAD_DIM = 64
CHUNK = 128
SSM_CH = 16
EPS = 1e-6
LANES = 128
BF16_ROWS = 16
NEG = -1e30
LOG2E = 1.4426950408889634
KV_TILES_PER_TRIP = 8

VMEM_LIMIT = 52 * 1024 * 1024
HIGHEST = lax.Precision.HIGHEST
NT_DIMS = (((1,), (1,)), ((), ()))


def _cparams(*sem):
    return pltpu.CompilerParams(dimension_semantics=sem, vmem_limit_bytes=VMEM_LIMIT)


def _rms(x):
    return x * lax.rsqrt(jnp.mean(x * x, axis=-1, keepdims=True) + EPS)


def _log_sigmoid(x):
    return jnp.minimum(x, 0.0) - jnp.log1p(jnp.exp(-jnp.abs(x)))


def _const_spec(shape):
    return pl.BlockSpec(shape, lambda *_: (0,) * len(shape), pipeline_mode=pl.Buffered(1))


def _split3_bf16(x):
    hi = x.astype(BF16).astype(F32)
    r = x - hi
    mid = r.astype(BF16).astype(F32)
    return hi, mid, (r - mid).astype(BF16).astype(F32)


def _proj_in_kernel(*refs, d_att, d_mlp, d_ssm, tm, prompt):
    if prompt:
        (h_ref, g_ref, w1_ref, w2t_ref, bfr_ref, bfc_ref, gqc_ref, gkc_ref, ws_ref, bs_ref,
         _kt_prev, _vt_prev, _lf_prev,
         qt_ref, kb_ref, aug_ref, tot_ref, kt_ref, vt_ref, vtb_ref, lf_ref, mlp_ref, su_ref) = refs
    else:
        (h_ref, g_ref, w1_ref, bfr_ref, gqr_ref, gkr_ref, gavg_ref, ws_ref, bs_ref,
         q_ref, k_ref, v_ref, lf_ref, mlp_ref, su_ref, mv_ref) = refs
    n_heads = d_att // HEAD_DIM
    xn = (_rms(h_ref[...]) * g_ref[...]).astype(BF16)
    z = jnp.dot(xn, w1_ref[...], preferred_element_type=F32)
    m0 = 0 if prompt else d_att
    mu = jax.nn.gelu(z[:, m0:m0 + d_mlp])
    mv = jax.nn.gelu(z[:, m0 + d_mlp:m0 + 2 * d_mlp])
    o = m0 + 2 * d_mlp + d_ssm
    su_ref[...] = z[:, m0 + 2 * d_mlp:o]
    lane = lax.broadcasted_iota(jnp.int32, (tm, LANES), 1)
    lf_tok = jnp.where(lane < n_heads, _log_sigmoid(z[:, o:o + LANES] + bfr_ref[...]), 0.0)

    def head_norm(t, g):
        ms = jnp.dot((t * t).astype(BF16), gavg_ref[...], preferred_element_type=F32)
        return t * lax.rsqrt(ms + EPS) * g

    def head_norm_t(t, g):
        t3 = t.reshape(n_heads, HEAD_DIM, tm)
        ms = jnp.mean(t3 * t3, axis=1, keepdims=True)
        return (t3 * lax.rsqrt(ms + EPS)).reshape(d_att, tm) * g

    if prompt:
        row = lax.broadcasted_iota(jnp.int32, (tm, LANES), 0)
        c_loc = lf_tok
        s = 1
        while s < tm:
            c_loc = c_loc + jnp.where(row >= s, pltpu.roll(c_loc, s, 0), 0.0)
            s *= 2
        hi, mid, lo = _split3_bf16(-LOG2E * c_loc)
        aug_ref[...] = (hi + pltpu.roll(mid, n_heads, 1) + pltpu.roll(lo, 2 * n_heads, 1)).astype(BF16)

        f = lax.dot_general(w2t_ref[...], xn, NT_DIMS, preferred_element_type=F32)
        qt_ref[0] = (head_norm_t(f[:d_att], gqc_ref[...]) * (LOG2E * HEAD_DIM ** -0.5)).astype(BF16)
        kn_t = head_norm_t(f[d_att:2 * d_att], gkc_ref[...])
        kt_ref[0, 0] = kn_t
        kb_ref[...] = kn_t.T.astype(BF16)
        vt = f[2 * d_att:3 * d_att]
        vt_ref[0, 0] = vt
        vtb_ref[0] = vt.astype(BF16)
        lf_t = _log_sigmoid(f[3 * d_att:3 * d_att + n_heads] + bfc_ref[...])
        lf_ref[0, 0] = lf_t
        tot_ref[0, 0] = jnp.broadcast_to(LOG2E * jnp.sum(lf_t, axis=1, keepdims=True), (n_heads, LANES))
    else:
        q_ref[...] = (head_norm(z[:, :d_att], gqr_ref[...]) * (HEAD_DIM ** -0.5)).astype(BF16)
        k_ref[...] = head_norm(z[:, o + LANES:o + LANES + d_att], gkr_ref[...])
        v_ref[...] = z[:, o + LANES + d_att:]
        lf_ref[...] = lf_tok.T[:n_heads, :]
        mv_ref[...] = mv

    n_mlp_heads = d_mlp // HEAD_DIM
    rows = lax.broadcasted_iota(jnp.int32, (CHUNK, CHUNK), 0)
    cols = lax.broadcasted_iota(jnp.int32, (CHUNK, CHUNK), 1)
    wst = [jnp.where(rows >= cols, ws_ref[h], 0.0).astype(BF16) for h in range(n_mlp_heads)]
    lane_head = lax.broadcasted_iota(jnp.int32, (CHUNK, d_mlp), 1) // HEAD_DIM
    for c in range(tm // CHUNK):
        sl = slice(c * CHUNK, (c + 1) * CHUNK)
        mvc = mv[sl].astype(BF16)
        mix = bs_ref[...]
        for h in range(n_mlp_heads):
            r = jnp.dot(wst[h], mvc, preferred_element_type=F32)
            mix = mix + jnp.where(lane_head == h, r, 0.0)
        mlp_ref[sl, :] = mu[sl] * mix


def _proj_in(h, consts, *, tm, d_att, d_mlp, d_ssm, seq_len=None, stacks=(), layer=None):
    n, d_model = h.shape
    n_heads = d_att // HEAD_DIM
    prompt = seq_len is not None
    assert n % tm == 0 and tm % CHUNK == 0
    row = lambda width: pl.BlockSpec((tm, width), lambda i: (i, 0))
    sds = jax.ShapeDtypeStruct
    aliases = {}
    if prompt:
        batch, nt = n // seq_len, seq_len // tm
        assert seq_len % tm == 0
        col = lambda height: pl.BlockSpec((1, height, tm), lambda i: (i // nt, 0, i % nt))
        stack = lambda height: pl.BlockSpec((1, 1, height, tm), lambda i: (layer, i // nt, 0, i % nt))
        fm = lambda dt: sds((batch, d_att, seq_len), dt)
        out_shape = [fm(BF16), sds((n, d_att), BF16), sds((n, LANES), BF16), sds((batch, nt, n_heads, LANES), F32),
                     sds(stacks[0].shape, F32), sds(stacks[1].shape, F32), fm(BF16), sds(stacks[2].shape, F32),
                     sds((n, d_mlp), F32), sds((n, d_ssm), F32)]
        out_specs = [col(d_att), row(d_att), row(LANES),
                     pl.BlockSpec((1, 1, n_heads, LANES), lambda i: (i // nt, i % nt, 0, 0)),
                     stack(d_att), stack(d_att), col(d_att), stack(n_heads), row(d_mlp), row(d_ssm)]
        first = 1 + len(consts)
        aliases = {first: 4, first + 1: 5, first + 2: 7}
    else:
        out_shape = [sds((n, d_att), BF16), sds((n, d_att), F32), sds((n, d_att), F32), sds((n_heads, n), F32),
                     sds((n, d_mlp), F32), sds((n, d_ssm), F32), sds((n, d_mlp), F32)]
        out_specs = [row(d_att), row(d_att), row(d_att), pl.BlockSpec((n_heads, tm), lambda i: (0, i)),
                     row(d_mlp), row(d_ssm), row(d_mlp)]
    return pl.pallas_call(
        functools.partial(_proj_in_kernel, d_att=d_att, d_mlp=d_mlp, d_ssm=d_ssm, tm=tm, prompt=prompt),
        grid=(n // tm,),
        in_specs=[row(d_model)] + [_const_spec(c.shape) for c in consts]
        + [pl.BlockSpec(memory_space=pl.ANY)] * len(stacks),
        out_specs=out_specs,
        out_shape=out_shape,
        input_output_aliases=aliases,
        compiler_params=_cparams("parallel"),
        name="proj_in",
    )(h, *consts, *stacks)


def _attn_prompt_kernel(qt_ref, k_ref, aug_ref, vt_ref, tot_ref, o_ref, m_sc, l_sc, acc_sc, s_sc, *, tq, n_heads):
    hp = pl.program_id(1)
    i = pl.program_id(2)
    qt = qt_ref[0].astype(F32)
    row = lax.broadcasted_iota(jnp.int32, (LANES, tq), 0)
    q_aug = []
    for e in range(2):
        h = 2 * hp + e
        own = (row < HEAD_DIM) if e == 0 else (row >= HEAD_DIM)
        ones_rows = (row == h) | (row == n_heads + h) | (row == 2 * n_heads + h)
        q_aug.append(jnp.concatenate([jnp.where(own, qt, 0.0), jnp.where(ones_rows, 1.0, 0.0)], axis=0).astype(BF16))

    m_sc[...] = jnp.full(m_sc.shape, NEG, F32)
    l_sc[...] = jnp.zeros(l_sc.shape, F32)
    acc_sc[...] = jnp.zeros(acc_sc.shape, F32)

    def produce(j, slot):
        k0 = pl.multiple_of(j * tq, tq)
        ka = jnp.concatenate([k_ref[pl.ds(k0, tq), :], aug_ref[pl.ds(k0, tq), :]], axis=1)
        for e in range(2):
            s_sc[slot, e] = jnp.dot(ka, q_aug[e], preferred_element_type=F32)

    ones_rows = jnp.ones((BF16_ROWS, tq), BF16)

    def consume(j, slot, refs, masked):
        k0 = pl.multiple_of(j * tq, tq)
        vt = vt_ref[0, :, pl.ds(k0, tq)]
        new_refs = []
        for e in range(2):
            s = s_sc[slot, e]
            if masked:
                kpos = lax.broadcasted_iota(jnp.int32, (tq, tq), 0)
                qpos = lax.broadcasted_iota(jnp.int32, (tq, tq), 1)
                s = jnp.where(kpos <= qpos, s, NEG)
            ref = jnp.tile(refs[e], (1, tq // LANES))
            m_old = m_sc[e]
            m_new = jnp.maximum(m_old, jnp.max(s, axis=0, keepdims=True) - ref)
            alpha = jnp.exp2(m_old - m_new)
            p = jnp.exp2(s - (m_new + ref)).astype(BF16)
            v_aug = jnp.concatenate([vt[e * HEAD_DIM:(e + 1) * HEAD_DIM, :], ones_rows], axis=0)
            pv = jnp.dot(v_aug, p, preferred_element_type=F32)
            l_sc[e] = alpha * l_sc[e] + pv[HEAD_DIM:HEAD_DIM + 1]
            acc_sc[e] = alpha * acc_sc[e] + pv[:HEAD_DIM]
            m_sc[e] = m_new
            new_refs.append(refs[e] + tot_ref[0, j, pl.ds(2 * hp + e, 1), :])
        return tuple(new_refs)

    produce(0, 0)

    def trip(jj, refs):
        j = KV_TILES_PER_TRIP * jj
        for k in range(KV_TILES_PER_TRIP):
            produce(j + k + 1, (k + 1) % 2)
            refs = consume(j + k, k % 2, refs, False)
        return refs

    zero_ref = jnp.zeros((1, LANES), F32)
    n_trips = i // KV_TILES_PER_TRIP
    refs = lax.fori_loop(0, n_trips, trip, (zero_ref, zero_ref))
    base = n_trips * KV_TILES_PER_TRIP

    for rest in range(KV_TILES_PER_TRIP):
        @pl.when(i - base == rest)
        def _(rest=rest):
            r = refs
            for k in range(rest):
                produce(base + k + 1, (k + 1) % 2)
                r = consume(base + k, k % 2, r, False)
            consume(i, rest % 2, r, True)

    out_t = jnp.concatenate([acc_sc[0] / l_sc[0], acc_sc[1] / l_sc[1]], axis=0)
    o_ref[...] = out_t.T


def _attn_prompt(qt_bf, k_bf, aug_bf, vt_bf, tot, *, tq):
    batch, d_att, seq_len = qt_bf.shape
    n_heads = d_att // HEAD_DIM
    n_pairs = d_att // LANES
    nq = seq_len // tq
    assert seq_len % tq == 0 and tq % LANES == 0 and tot.shape == (batch, nq, n_heads, LANES)
    return pl.pallas_call(
        functools.partial(_attn_prompt_kernel, tq=tq, n_heads=n_heads),
        grid=(batch, n_pairs, nq),
        in_specs=[pl.BlockSpec((1, LANES, tq), lambda b, hp, i: (b, hp, i)),
                  pl.BlockSpec((seq_len, LANES), lambda b, hp, i: (b, hp)),
                  pl.BlockSpec((seq_len, LANES), lambda b, hp, i: (b, 0)),
                  pl.BlockSpec((1, LANES, seq_len), lambda b, hp, i: (b, hp, 0)),
                  pl.BlockSpec((1, nq, n_heads, LANES), lambda b, hp, i: (b, 0, 0, 0))],
        out_specs=pl.BlockSpec((tq, LANES), lambda b, hp, i: (b * nq + i, hp)),
        out_shape=jax.ShapeDtypeStruct((batch * seq_len, d_att), F32),
        scratch_shapes=[pltpu.VMEM((2, 1, tq), F32), pltpu.VMEM((2, 1, tq), F32), pltpu.VMEM((2, HEAD_DIM, tq), F32),
                        pltpu.VMEM((2, 2, tq, tq), F32)],
        compiler_params=_cparams("parallel", "parallel", "parallel"),
        name="attn_prompt",
    )(qt_bf, k_bf, aug_bf, vt_bf, tot)


def _attn_sample_kernel(pt_ref, q_ref, kn_ref, vn_ref, lfn_ref, *refs, pages_per_step, n_new):
    del pt_ref
    pg = pages_per_step
    k_refs, v_refs, lf_refs = refs[:pg], refs[pg:2 * pg], refs[2 * pg:3 * pg]
    o_ref, qbd_sc, m_sc, l_sc, acc_sc, coff_sc = refs[3 * pg:]
    g = pl.program_id(1)
    n_rows, d_att = qbd_sc.shape
    n_heads = d_att // HEAD_DIM
    row_head = lax.broadcasted_iota(jnp.int32, (n_rows, d_att), 0) % n_heads
    lane_head = lax.broadcasted_iota(jnp.int32, (n_rows, d_att), 1) // HEAD_DIM
    own = row_head == lane_head

    @pl.when(g == 0)
    def _init():
        qf = q_ref[0].astype(F32)
        rows = jnp.concatenate([jnp.broadcast_to(qf[i:i + 1], (n_heads, d_att)) for i in range(n_new)], axis=0)
        qbd_sc[...] = jnp.where(own, rows, 0.0).astype(BF16)
        m_sc[...] = jnp.full(m_sc.shape, NEG, F32)
        l_sc[...] = jnp.zeros(l_sc.shape, F32)
        acc_sc[...] = jnp.zeros(acc_sc.shape, F32)
        coff_sc[...] = jnp.zeros(coff_sc.shape, F32)

    lf = jnp.concatenate([lf_refs[p][0, 0] for p in range(pg)], axis=0)
    tri = (lax.broadcasted_iota(jnp.int32, (CHUNK, CHUNK), 0)
           <= lax.broadcasted_iota(jnp.int32, (CHUNK, CHUNK), 1)).astype(BF16)
    c_in = sum(jnp.dot(t.astype(BF16), tri, preferred_element_type=F32) for t in _split3_bf16(lf))
    tot = jnp.broadcast_to(c_in[:, CHUNK - 1:CHUNK], c_in.shape)
    page = lax.broadcasted_iota(jnp.int32, c_in.shape, 0) // n_heads
    incl = tot
    s = 1
    while s < pg:
        incl = incl + jnp.where(page >= s, pltpu.roll(incl, s * n_heads, 0), 0.0)
        s *= 2
    coff = coff_sc[...]
    ck = c_in + (incl - tot) + jnp.concatenate([coff] * pg, axis=0)
    coff_sc[...] = coff + incl[(pg - 1) * n_heads:, :]

    qbd = qbd_sc[...]
    s_parts = []
    for p in range(pg):
        ck_p = jnp.concatenate([ck[p * n_heads:(p + 1) * n_heads]] * n_new, axis=0)
        s_parts.append(jnp.dot(qbd, k_refs[p][0, 0].astype(BF16), preferred_element_type=F32) - ck_p)
    s_all = jnp.concatenate(s_parts, axis=1)
    m_old = m_sc[...]
    m_new = jnp.maximum(m_old, jnp.max(s_all, axis=1, keepdims=True))
    alpha = jnp.exp(m_old - m_new)
    p_all = jnp.exp(s_all - m_new)
    l_sc[...] = alpha * l_sc[...] + jnp.sum(p_all, axis=1, keepdims=True)
    pv = jnp.zeros(acc_sc.shape, F32)
    for p in range(pg):
        pv = pv + lax.dot_general(p_all[:, p * CHUNK:(p + 1) * CHUNK].astype(BF16), v_refs[p][0, 0].astype(BF16),
                                  NT_DIMS, preferred_element_type=F32)
    acc_sc[...] = alpha * acc_sc[...] + pv
    m_sc[...] = m_new

    @pl.when(g == pl.num_programs(1) - 1)
    def _finish():
        qf = qbd_sc[...].astype(F32)
        row_query = lax.broadcasted_iota(jnp.int32, (n_rows, 1), 0) // n_heads
        lfn = jnp.concatenate([lfn_ref[0]] * n_new, axis=0)
        c = jnp.concatenate([coff_sc[...]] * n_new, axis=0)[:, 0:1]
        m, l, acc = m_sc[...], l_sc[...], acc_sc[...]
        for j in range(n_new):
            c = c + lfn[:, j:j + 1]
            s = jnp.sum(qf * kn_ref[0, j:j + 1, :], axis=1, keepdims=True) - c
            s = jnp.where(row_query >= j, s, NEG)
            m_new = jnp.maximum(m, s)
            alpha = jnp.exp(m - m_new)
            pj = jnp.exp(s - m_new)
            l = alpha * l + pj
            acc = alpha * acc + pj * vn_ref[0, j:j + 1, :]
            m = m_new
        out = jnp.where(own, acc / l, 0.0)
        for i in range(n_new):
            o_ref[0, i:i + 1, :] = jnp.sum(out[i * n_heads:(i + 1) * n_heads], axis=0, keepdims=True)


def _attn_sample(page_table, q_bf, k_new, v_new, lf_new, cache_kt, cache_vt, cache_lft, *, layer, pages_per_step):
    n_seq, n_new, d_att = q_bf.shape
    n_heads = d_att // HEAD_DIM
    n_pages = page_table.shape[1]
    pg = pages_per_step
    assert n_pages % pg == 0 and cache_kt.shape[2:] == (d_att, CHUNK) and cache_lft.shape[2:] == (n_heads, CHUNK)
    n_rows = n_new * n_heads

    def page_spec(height, p):
        return pl.BlockSpec((1, 1, height, CHUNK), lambda s, g, pt: (layer, pt[s, g * pg + p], 0, 0))

    seq_spec = lambda a: pl.BlockSpec((1,) + a.shape[1:], lambda s, g, pt: (s, 0, 0))
    grid_spec = pltpu.PrefetchScalarGridSpec(
        num_scalar_prefetch=1,
        grid=(n_seq, n_pages // pg),
        in_specs=[seq_spec(q_bf), seq_spec(k_new), seq_spec(v_new), seq_spec(lf_new)]
        + [page_spec(d_att, p) for p in range(pg)]
        + [page_spec(d_att, p) for p in range(pg)]
        + [page_spec(n_heads, p) for p in range(pg)],
        out_specs=pl.BlockSpec((1, n_new, d_att), lambda s, g, pt: (s, 0, 0)),
        scratch_shapes=[pltpu.VMEM((n_rows, d_att), BF16), pltpu.VMEM((n_rows, 1), F32), pltpu.VMEM((n_rows, 1), F32),
                        pltpu.VMEM((n_rows, d_att), F32), pltpu.VMEM((n_heads, CHUNK), F32)],
    )
    return pl.pallas_call(
        functools.partial(_attn_sample_kernel, pages_per_step=pg, n_new=n_new),
        grid_spec=grid_spec,
        out_shape=jax.ShapeDtypeStruct((n_seq, n_new, d_att), F32),
        compiler_params=_cparams("parallel", "arbitrary"),
        name="attn_sample",
    )(page_table, q_bf, k_new, v_new, lf_new, *([cache_kt] * pg), *([cache_vt] * pg), *([cache_lft] * pg))


def _s5_glu(y, wg_ref, bg_ref):
    z = jax.nn.gelu(y)
    gate = jax.nn.sigmoid(jnp.dot(z.astype(BF16), wg_ref[...], preferred_element_type=F32) + bg_ref[...])
    return z * gate


def _s5_prompt_kernel(u_ref, h0_ref, ab_ref, bb_ref, cc_ref, d_ref, wg_ref, bg_ref, o_ref, hT_ref,
                      h_sc, bu_sc, hs_sc, *, ts, n_state):
    t = pl.program_id(1)

    @pl.when(t == 0)
    def _():
        h_sc[...] = h0_ref[0]

    u = u_ref[...]
    bu_sc[...] = jnp.dot(u.astype(BF16), bb_ref[...], preferred_element_type=F32)
    a_re = ab_ref[0:1, :]
    a_im = ab_ref[1:2, :]

    def body(tt, carry):
        h_re, h_im = carry
        b_re = bu_sc[pl.ds(tt, 1), :n_state]
        b_im = bu_sc[pl.ds(tt, 1), n_state:]
        n_re = a_re * h_re - a_im * h_im + b_re
        n_im = a_re * h_im + a_im * h_re + b_im
        hs_sc[pl.ds(tt, 1), :n_state] = n_re
        hs_sc[pl.ds(tt, 1), n_state:] = n_im
        return n_re, n_im

    h_re, h_im = lax.fori_loop(0, ts, body, (h_sc[:, :n_state], h_sc[:, n_state:]), unroll=8)
    h_sc[:, :n_state] = h_re
    h_sc[:, n_state:] = h_im
    y = jnp.dot(hs_sc[...].astype(BF16), cc_ref[...], preferred_element_type=F32) + d_ref[...] * u
    o_ref[...] = _s5_glu(y, wg_ref, bg_ref)

    @pl.when(t == pl.num_programs(1) - 1)
    def _():
        hT_ref[0] = h_sc[...]


def _s5_prompt(u, h0, ab, bb, cc, d, wg, bg, *, batch, seq_len, ts):
    n, d_ssm = u.shape
    n_state = ab.shape[1]
    nt = seq_len // ts
    assert seq_len % ts == 0
    consts = [ab, bb, cc, d, wg, bg]
    return pl.pallas_call(
        functools.partial(_s5_prompt_kernel, ts=ts, n_state=n_state),
        grid=(batch, nt),
        in_specs=[pl.BlockSpec((ts, d_ssm), lambda b, t: (b * nt + t, 0)),
                  pl.BlockSpec((1, 1, 2 * n_state), lambda b, t: (b, 0, 0))]
        + [_const_spec(c.shape) for c in consts],
        out_specs=[pl.BlockSpec((ts, d_ssm), lambda b, t: (b * nt + t, 0)),
                   pl.BlockSpec((1, 1, 2 * n_state), lambda b, t: (b, 0, 0))],
        out_shape=[jax.ShapeDtypeStruct((n, d_ssm), F32), jax.ShapeDtypeStruct((batch, 1, 2 * n_state), F32)],
        scratch_shapes=[pltpu.VMEM((1, 2 * n_state), F32), pltpu.VMEM((ts, 2 * n_state), F32),
                        pltpu.VMEM((ts, 2 * n_state), F32)],
        compiler_params=_cparams("parallel", "arbitrary"),
        name="s5_prompt",
    )(u, h0, *consts)


def _s5_sample_kernel(u_ref, h0_ref, ab_ref, bb_ref, cc_ref, d_ref, wg_ref, bg_ref, o_ref, hT_ref, *, n_state):
    a_re = ab_ref[0:1, :]
    a_im = ab_ref[1:2, :]
    h_re = h0_ref[:, :n_state]
    h_im = h0_ref[:, n_state:]
    for t in range(u_ref.shape[0]):
        u = u_ref[t]
        bu = jnp.dot(u, bb_ref[...], precision=HIGHEST, preferred_element_type=F32)
        h_re, h_im = (a_re * h_re - a_im * h_im + bu[:, :n_state],
                      a_re * h_im + a_im * h_re + bu[:, n_state:])
        hs = jnp.concatenate([h_re, h_im], axis=1).astype(BF16)
        y = jnp.dot(hs, cc_ref[...], preferred_element_type=F32) + d_ref[...] * u
        o_ref[t] = _s5_glu(y, wg_ref, bg_ref)
    hT_ref[:, :n_state] = h_re
    hT_ref[:, n_state:] = h_im


def _s5_sample(u, h0, ab, bb_f32, cc, d, wg, bg):
    n_new, n_seq, d_ssm = u.shape
    n_state = ab.shape[1]
    return pl.pallas_call(
        functools.partial(_s5_sample_kernel, n_state=n_state),
        out_shape=[jax.ShapeDtypeStruct((n_new, n_seq, d_ssm), F32), jax.ShapeDtypeStruct((n_seq, 2 * n_state), F32)],
        compiler_params=pltpu.CompilerParams(vmem_limit_bytes=VMEM_LIMIT),
        name="s5_sample",
    )(u, h0, ab, bb_f32, cc, d, wg, bg)


def _post_kernel(h_ref, att_ref, mlp_ref, ssm_ref, pe_ref, gmo_ref, wo_ref, gf_ref, wg_ref, wu_ref, wd_ref,
                 gp_ref, wpg_ref, wpp_ref, o_ref, *, d_att, d_mlp):
    gmo = gmo_ref[...]
    e1 = d_att + d_mlp
    a = (_rms(att_ref[...]) * gmo[:, :d_att]).astype(BF16)
    m = (_rms(mlp_ref[...]) * gmo[:, d_att:e1]).astype(BF16)
    s = (_rms(ssm_ref[...]) * gmo[:, e1:]).astype(BF16)
    h = h_ref[...] + (jnp.dot(a, wo_ref[:d_att, :], preferred_element_type=F32)
                      + jnp.dot(m, wo_ref[d_att:e1, :], preferred_element_type=F32)
                      + jnp.dot(s, wo_ref[e1:, :], preferred_element_type=F32))
    hn = (_rms(h) * gf_ref[...]).astype(BF16)
    ff = jax.nn.silu(jnp.dot(hn, wg_ref[...], preferred_element_type=F32)) * jnp.dot(
        hn, wu_ref[...], preferred_element_type=F32)
    h = h + jnp.dot(ff.astype(BF16), wd_ref[...], preferred_element_type=F32)
    gate = jax.nn.sigmoid(jnp.dot((_rms(h) * gp_ref[...]).astype(BF16), wpg_ref[...], preferred_element_type=F32))
    o_ref[...] = h + jnp.dot(pe_ref[0].astype(BF16), wpp_ref[...], preferred_element_type=F32) * gate


def _post(h, att, mlp, ssm, pe, gmo, wo, gf, wg, wu, wd, gp, wpg, wpp, *, tm, pe_layer):
    n, d_model = h.shape
    assert n % tm == 0
    row = lambda a: pl.BlockSpec((tm, a.shape[1]), lambda i: (i, 0))
    consts = [gmo, wo, gf, wg, wu, wd, gp, wpg, wpp]
    return pl.pallas_call(
        functools.partial(_post_kernel, d_att=att.shape[1], d_mlp=mlp.shape[1]),
        grid=(n // tm,),
        in_specs=[row(h), row(att), row(mlp), row(ssm), pl.BlockSpec((1, tm, pe.shape[2]), lambda i: (pe_layer, i, 0))]
        + [_const_spec(c.shape) for c in consts],
        out_specs=row(h),
        out_shape=jax.ShapeDtypeStruct((n, d_model), F32),
        compiler_params=_cparams("parallel"),
        name="post",
    )(h, att, mlp, ssm, pe, *consts)


def _block_diag(blocks):
    g, r, c = blocks.shape
    eye = jnp.eye(g, dtype=blocks.dtype)
    return (eye[:, None, :, None] * blocks[:, :, None, :]).reshape(g * r, g * c)


def _s5_params(a_re, a_im, log_dt, b_re, b_im, c_re, c_im, d, w_glu, b_glu):
    dt = jnp.exp(log_dt)[:, None]
    mag = jnp.exp(a_re * dt)
    ab_re = mag * jnp.cos(a_im * dt)
    ab_im = mag * jnp.sin(a_im * dt)
    den = a_re * a_re + a_im * a_im
    coef_re = ((ab_re - 1.0) * a_re + ab_im * a_im) / den
    coef_im = (ab_im * a_re - (ab_re - 1.0) * a_im) / den
    bb_re = coef_re[..., None] * b_re - coef_im[..., None] * b_im
    bb_im = coef_re[..., None] * b_im + coef_im[..., None] * b_re
    ab = jnp.stack([ab_re.reshape(-1), ab_im.reshape(-1)])
    bb = jnp.concatenate([_block_diag(jnp.swapaxes(bb_re, 1, 2)), _block_diag(jnp.swapaxes(bb_im, 1, 2))], axis=1)
    cc = jnp.concatenate([_block_diag(jnp.swapaxes(c_re, 1, 2)), -_block_diag(jnp.swapaxes(c_im, 1, 2))], axis=0)
    return ab, bb, cc.astype(BF16), d[None, :], _block_diag(w_glu).astype(BF16), b_glu[None, :]


def kernel(x_prompt, x_sample, cache_k, cache_v, cache_logf, state_ssm_re, state_ssm_im, page_table, p_prompt, p_sample, g_norm_mix, w_in, b_forget, g_q, g_k, w_spatial, b_spatial, ssm_a_re, ssm_a_im, ssm_log_dt, ssm_b_re, ssm_b_im, ssm_c_re, ssm_c_im, ssm_d, w_glu, b_glu, g_mix_out, w_out, g_norm_ffn, w_ffn_gate, w_ffn_up, w_ffn_down, g_norm_ple, w_ple_gate, w_ple_proj):
    batch, seq_len, d_model = x_prompt.shape
    n_seq, n_new, _ = x_sample.shape
    depth, n_pool, page, n_heads, head_dim = cache_k.shape
    assert head_dim == HEAD_DIM and page == CHUNK and n_seq * n_new == CHUNK
    d_att = n_heads * HEAD_DIM
    n_mlp_heads = w_spatial.shape[1]
    d_mlp = n_mlp_heads * HEAD_DIM
    n_groups, n_st = ssm_a_re.shape[1:]
    d_ssm = n_groups * SSM_CH
    n_state = n_groups * n_st
    n_p = batch * seq_len
    n_s = n_seq * n_new

    tq = min(512, seq_len)
    ts = min(512, seq_len)
    tm_post = min(512, seq_len)
    pages_per_step = min(16, page_table.shape[1])

    cache_kt = jnp.transpose(cache_k, (0, 1, 3, 4, 2)).reshape(depth, n_pool, d_att, page)
    cache_vt = jnp.transpose(cache_v, (0, 1, 3, 4, 2)).reshape(depth, n_pool, d_att, page)
    cache_lft = jnp.swapaxes(cache_logf, 2, 3)

    gavg = _block_diag(jnp.full((n_heads, HEAD_DIM, HEAD_DIM), 1.0 / HEAD_DIM, F32)).astype(BF16)
    tril_new = jnp.tril(jnp.ones((n_new, n_new), F32))
    eye_seq = jnp.eye(n_seq, dtype=F32)

    hp = x_prompt.reshape(n_p, d_model)
    hs = x_sample.reshape(n_s, d_model)
    h0_prompt = jnp.zeros((batch, 1, 2 * n_state), F32)
    pe_prompt = p_prompt.reshape(depth, n_p, -1)
    pe_sample = p_sample.reshape(depth, n_s, -1)
    outs = {name: [] for name in ("srp", "sip", "ks", "vs", "lfs", "srs", "sis", "mvs")}
    kt_all = jnp.zeros((depth, batch, d_att, seq_len), F32)
    vt_all = jnp.zeros((depth, batch, d_att, seq_len), F32)
    lf_all = jnp.zeros((depth, batch, n_heads, seq_len), F32)

    f0, f1 = 3 * d_att, 3 * d_att + n_heads
    w_q, w_k, w_v, w_f, w_rest = (w_in[:, :, :d_att], w_in[:, :, d_att:2 * d_att], w_in[:, :, 2 * d_att:f0],
                                  w_in[:, :, f0:f1], w_in[:, :, f1:])
    w_f_pad = jnp.concatenate([w_f, jnp.zeros((depth, d_model, LANES - n_heads), F32)], axis=2)
    w1_p_all = jnp.concatenate([w_rest, w_f_pad], axis=2).astype(BF16)
    pad_rows = -(3 * d_att + n_heads) % BF16_ROWS
    w2t_p_all = jnp.concatenate([jnp.swapaxes(w, 1, 2) for w in (w_q, w_k, w_v, w_f)]
                                + [jnp.zeros((depth, pad_rows, d_model), F32)], axis=1).astype(BF16)
    w1_s_all = jnp.concatenate([w_q, w_rest, w_f_pad, w_k, w_v], axis=2).astype(BF16)
    bf_row_all = jnp.concatenate([b_forget, jnp.zeros((depth, LANES - n_heads), F32)], axis=1)[:, None, :]
    gq_all = jnp.tile(g_q, (1, n_heads))
    gk_all = jnp.tile(g_k, (1, n_heads))
    bs_p_all = jnp.repeat(jnp.swapaxes(b_spatial, 1, 2), HEAD_DIM, axis=2)
    ws_s_all = jnp.einsum("st,lhij->lhsitj", eye_seq, w_spatial[:, :, :n_new, :n_new] * tril_new).reshape(
        depth, n_mlp_heads, n_s, n_s)
    bs_s_all = jnp.tile(jnp.repeat(jnp.swapaxes(b_spatial[:, :, :n_new], 1, 2), HEAD_DIM, axis=2), (1, n_seq, 1))
    s5_all = jax.vmap(_s5_params)(ssm_a_re, ssm_a_im, ssm_log_dt, ssm_b_re, ssm_b_im, ssm_c_re, ssm_c_im,
                                  ssm_d, w_glu, b_glu)
    bb_bf_all = s5_all[1].astype(BF16)
    post_all = (g_mix_out[:, None, :], w_out.astype(BF16), g_norm_ffn[:, None, :], w_ffn_gate.astype(BF16),
                w_ffn_up.astype(BF16), w_ffn_down.astype(BF16), g_norm_ple[:, None, :],
                w_ple_gate.astype(BF16), w_ple_proj.astype(BF16))
    dims = dict(d_att=d_att, d_mlp=d_mlp, d_ssm=d_ssm)

    for i in range(depth):
        w1_p, w2t_p, w1_s, bf_row = w1_p_all[i], w2t_p_all[i], w1_s_all[i], bf_row_all[i]
        gq, gk, bs_p, ws_s, bs_s = gq_all[i], gk_all[i], bs_p_all[i], ws_s_all[i], bs_s_all[i]
        ab, bb, cc, d_row, wglu_bd, bglu = (a[i] for a in s5_all)
        bb_bf = bb_bf_all[i]
        post_w = tuple(a[i] for a in post_all)
        g_mix = g_norm_mix[i][None, :]

        qt_bf, k_bf, aug_bf, tot, kt_all, vt_all, vt_bf, lf_all, mlp, su = _proj_in(
            hp, [g_mix, w1_p, w2t_p, bf_row, b_forget[i][:, None], gq[:, None], gk[:, None], w_spatial[i], bs_p],
            tm=tq, seq_len=seq_len, stacks=(kt_all, vt_all, lf_all), layer=i, **dims)
        att = _attn_prompt(qt_bf, k_bf, aug_bf, vt_bf, tot, tq=tq)
        ssm, h_t = _s5_prompt(su, h0_prompt, ab, bb_bf, cc, d_row, wglu_bd, bglu,
                              batch=batch, seq_len=seq_len, ts=ts)
        hp = _post(hp, att, mlp, ssm, pe_prompt, *post_w, tm=tm_post, pe_layer=i)
        outs["srp"].append(h_t[:, 0, :n_state].reshape(batch, n_groups, n_st))
        outs["sip"].append(h_t[:, 0, n_state:].reshape(batch, n_groups, n_st))

        q_bf, k, v, lf_t, mlp, su, mv = _proj_in(
            hs, [g_mix, w1_s, bf_row, gq[None, :], gk[None, :], gavg, ws_s, bs_s], tm=n_s, **dims)
        lf_new = jnp.swapaxes(lf_t.reshape(n_heads, n_seq, n_new), 0, 1)
        att = _attn_sample(page_table, q_bf.reshape(n_seq, n_new, d_att), k.reshape(n_seq, n_new, d_att),
                           v.reshape(n_seq, n_new, d_att), lf_new, cache_kt, cache_vt, cache_lft,
                           layer=i, pages_per_step=pages_per_step)
        h0_s = jnp.concatenate([state_ssm_re[i].reshape(n_seq, n_state), state_ssm_im[i].reshape(n_seq, n_state)], axis=1)
        ssm, h_t = _s5_sample(jnp.swapaxes(su.reshape(n_seq, n_new, d_ssm), 0, 1), h0_s, ab, bb, cc, d_row, wglu_bd, bglu)
        ssm = jnp.swapaxes(ssm, 0, 1).reshape(n_s, d_ssm)
        hs = _post(hs, att.reshape(n_s, d_att), mlp, ssm, pe_sample, *post_w, tm=n_s, pe_layer=i)
        outs["ks"].append(k.reshape(n_seq, n_new, n_heads, HEAD_DIM))
        outs["vs"].append(v.reshape(n_seq, n_new, n_heads, HEAD_DIM))
        outs["lfs"].append(lf_t.T.reshape(n_seq, n_new, n_heads))
        outs["srs"].append(h_t[:, :n_state].reshape(n_seq, n_groups, n_st))
        outs["sis"].append(h_t[:, n_state:].reshape(n_seq, n_groups, n_st))
        outs["mvs"].append(mv.reshape(n_seq, n_new, d_mlp))

    st = {name: jnp.stack(v) for name, v in outs.items()}
    to_heads = lambda a: jnp.transpose(a.reshape(depth, batch, n_heads, HEAD_DIM, seq_len), (0, 1, 4, 2, 3))
    return (hp.reshape(batch, seq_len, d_model), hs.reshape(n_seq, n_new, d_model),
            to_heads(kt_all), to_heads(vt_all), jnp.swapaxes(lf_all, 2, 3), st["srp"], st["sip"],
            st["ks"], st["vs"], st["lfs"], st["srs"], st["sis"], st["mvs"])
```
